```python
import math, functools
import jax, jax.numpy as jnp
from jax import lax
import numpy as np

D_MODEL = 1024
BATCH = 8
SEQ = 2048
DEPTH = 1
DEC_BATCH = 128
DEC_SEQ = 4
PAST_LEN = 8192
PAGE_SIZE = 128

HEAD_DIM = 64
MIX_WIDTH = 2 * D_MODEL
SB_WIDTH = MIX_WIDTH // 4
SB_HEADS = SB_WIDTH // HEAD_DIM
SSM_WIDTH = MIX_WIDTH - SB_WIDTH
SSM_HEADS = SSM_WIDTH // HEAD_DIM
SSM_GROUPS = 4
D_STATE = 128
SSM_CONV = 4
CONV_DIM = SSM_WIDTH + 2 * SSM_GROUPS * D_STATE
SSD_CHUNK = 128
SB_BLOCK = 128
SB_SCALE = HEAD_DIM ** -0.5
SB_BIAS_INIT = -9.0
D_FF = 2816
FFN_CONV = 3
EPS = 1e-6
IN_SPLITS = (SB_WIDTH, 2 * SB_WIDTH, 3 * SB_WIDTH, 3 * SB_WIDTH + SSM_WIDTH,
             3 * SB_WIDTH + SSM_WIDTH + CONV_DIM)
IN_DIM = 3 * SB_WIDTH + SSM_WIDTH + CONV_DIM + SSM_HEADS

kernel_name = 'stick_breaking_ssd_hybrid_step'


def rmsnorm(x, g):
    xf = x.astype(jnp.float32)
    y = xf * lax.rsqrt(jnp.mean(xf * xf, axis=-1, keepdims=True) + EPS)
    return (y * g.astype(jnp.float32)).astype(x.dtype)


def causal_dwconv(x, prev, w, b):
    width = w.shape[0]
    length = x.shape[1]
    xp = jnp.concatenate([prev.astype(x.dtype), x], axis=1)
    out = b
    for tap in range(width):
        out = out + xp[:, tap:tap + length] * w[tap]
    return out, xp[:, -(width - 1):]


def sb_segment(q, k, v, bias, carry, mask):
    z = jnp.einsum('bthd,bshd->bhts', q, k).astype(jnp.float32) + bias.astype(jnp.float32)[None, :, None, None]
    l = jax.nn.log_sigmoid(-z)
    if mask is not None:
        l = jnp.where(mask, l, 0.0)
    suffix = lax.cumsum(l, axis=3, reverse=True) - l
    w = jnp.exp(jax.nn.log_sigmoid(z) + suffix + carry[..., None])
    if mask is not None:
        w = jnp.where(mask, w, 0.0)
    o = jnp.einsum('bhts,bshd->bthd', w, v.astype(jnp.float32))
    return o, carry + jnp.sum(l, axis=3)


def sb_prompt(q, k, v, bias):
    bsz, seq, heads, hd = q.shape
    nb = seq // SB_BLOCK
    qb = jnp.moveaxis(q.reshape(bsz, nb, SB_BLOCK, heads, hd), 1, 0)
    kpos = jnp.arange(seq)

    def block(args):
        qi, i = args
        qpos = i * SB_BLOCK + jnp.arange(SB_BLOCK)
        mask = kpos[None, :] < qpos[:, None]
        o, _ = sb_segment(qi, k, v, bias, jnp.zeros((bsz, heads, SB_BLOCK), jnp.float32), mask)
        return o

    o = lax.map(block, (qb, jnp.arange(nb)))
    return jnp.moveaxis(o, 0, 1).reshape(bsz, seq, heads, hd)


def sb_sample(q, k, v, bias, cache_k, cache_v, page_table, layer):
    bsz, t, heads, hd = q.shape
    mask = jnp.arange(t)[None, :] < jnp.arange(t)[:, None]
    o, carry = sb_segment(q, k, v, bias, jnp.zeros((bsz, heads, t), jnp.float32), mask)

    def page_step(state, pages):
        o_acc, c = state
        kp = cache_k[pages, :, layer]
        vp = cache_v[pages, :, layer]
        o_p, c = sb_segment(q, kp, vp, bias, c, None)
        return (o_acc + o_p, c), None

    (o, _), _ = lax.scan(page_step, (o, carry), page_table.T, reverse=True)
    return o


def ssd_chunked(x, dt, a, b_in, c_in, h0):
    bsz, length, heads, hd = x.shape
    groups, nst = b_in.shape[2], b_in.shape[3]
    e = heads // groups
    cl = SSD_CHUNK if length % SSD_CHUNK == 0 else length
    nc = length // cl
    xdt = (x * dt[..., None]).reshape(bsz, nc, cl, groups, e, hd)
    da = (dt * a).reshape(bsz, nc, cl, groups, e)
    bc = b_in.reshape(bsz, nc, cl, groups, nst)
    cc = c_in.reshape(bsz, nc, cl, groups, nst)
    acum = jnp.cumsum(da, axis=2)
    causal = jnp.tril(jnp.ones((cl, cl), bool))[None, None, :, :, None, None]
    seg = acum[:, :, :, None] - acum[:, :, None, :]
    lmat = jnp.exp(jnp.where(causal, seg, -jnp.inf))
    cb = jnp.einsum('bctgn,bcsgn->bctsg', cc, bc)
    y_diag = jnp.einsum('bctsg,bctsge,bcsgep->bctgep', cb, lmat, xdt)
    decay_to_end = jnp.exp(acum[:, :, -1:] - acum)
    chunk_states = jnp.einsum('bclgn,bclge,bclgep->bcgepn', bc, decay_to_end, xdt)
    chunk_decay = jnp.exp(acum[:, :, -1])

    def step(h, inp):
        dec, st = inp
        return dec[..., None, None] * h + st, h

    h_final, h_in = lax.scan(step, h0.reshape(bsz, groups, e, hd, nst),
                             (jnp.moveaxis(chunk_decay, 1, 0), jnp.moveaxis(chunk_states, 1, 0)))
    h_in = jnp.moveaxis(h_in, 0, 1)
    y_off = jnp.einsum('bclgn,bcgepn,bclge->bclgep', cc, h_in, jnp.exp(acum))
    y = (y_diag + y_off).reshape(bsz, length, heads, hd)
    return y, h_final.reshape(bsz, heads, hd, nst)


def layer_forward(x, lw, attend, h0, conv_prev, ffn_prev):
    bsz, length, _ = x.shape
    f32 = jnp.float32
    h = rmsnorm(x, lw['g_mix'])
    proj = h @ lw['w_in']
    q, k, v, z, xbc, dt = jnp.split(proj, IN_SPLITS, axis=-1)
    q = q.reshape(bsz, length, SB_HEADS, HEAD_DIM) * SB_SCALE
    k = k.reshape(bsz, length, SB_HEADS, HEAD_DIM)
    v = v.reshape(bsz, length, SB_HEADS, HEAD_DIM)
    o_sb = attend(q, k, v, lw['sb_bias']).reshape(bsz, length, SB_WIDTH).astype(x.dtype)
    o_sb = rmsnorm(o_sb, lw['g_sb_out'])
    xbc_c, conv_new = causal_dwconv(xbc, conv_prev, lw['w_conv'], lw['b_conv'])
    xbc_c = jax.nn.silu(xbc_c)
    xs, bm, cm = jnp.split(xbc_c, (SSM_WIDTH, SSM_WIDTH + SSM_GROUPS * D_STATE), axis=-1)
    xs = xs.reshape(bsz, length, SSM_HEADS, HEAD_DIM).astype(f32)
    bm = bm.reshape(bsz, length, SSM_GROUPS, D_STATE).astype(f32)
    cm = cm.reshape(bsz, length, SSM_GROUPS, D_STATE).astype(f32)
    dtp = jax.nn.softplus((dt + lw['dt_bias']).astype(f32))
    a = -jnp.exp(lw['a_log'].astype(f32))
    y, h_new = ssd_chunked(xs, dtp, a, bm, cm, h0.astype(f32))
    y = y + lw['d_skip'].astype(f32)[:, None] * xs
    y = y.reshape(bsz, length, SSM_WIDTH) * jax.nn.silu(z.astype(f32))
    y = rmsnorm(y.reshape(bsz, length, SSM_GROUPS, SSM_WIDTH // SSM_GROUPS),
                lw['g_ssm'].reshape(SSM_GROUPS, SSM_WIDTH // SSM_GROUPS))
    y = y.reshape(bsz, length, SSM_WIDTH).astype(x.dtype)
    x = x + jnp.concatenate([o_sb, y], axis=-1) @ lw['w_out']
    hf = rmsnorm(x, lw['g_ffn'])
    gate, val = jnp.split(hf @ lw['w_up'], 2, axis=-1)
    gate_c, ffn_new = causal_dwconv(gate, ffn_prev, lw['w_ffn_conv'], lw['b_ffn_conv'])
    x = x + (jax.nn.silu(gate_c) * val) @ lw['w_down']
    return x, (k, v, h_new, conv_new, ffn_new)


def setup_inputs(seed: int = 0) -> dict:
    key = jax.random.key(seed)
    ks = jax.random.split(key, 25)
    f32 = jnp.float32
    n_pages = PAST_LEN // PAGE_SIZE
    n_used = DEC_BATCH * n_pages
    n_pool = n_used + n_used // 4

    def nrm(k, shape, scale):
        return jax.random.normal(k, shape, f32) * scale

    def gain(k, shape):
        return 1.0 + 0.02 * jax.random.normal(k, shape, f32)

    dt0 = jnp.exp(jax.random.uniform(ks[13], (DEPTH, SSM_HEADS), f32, math.log(1e-3), math.log(1e-1)))
    dt_bias = dt0 + jnp.log(-jnp.expm1(-dt0))
    a_log = jnp.log(jax.random.uniform(ks[14], (DEPTH, SSM_HEADS), f32, 1.0, 16.0))
    page_table = jax.random.permutation(ks[7], n_pool)[:n_used].reshape(DEC_BATCH, n_pages).astype(jnp.int32)
    return {
        'x_prompt': nrm(ks[0], (BATCH, SEQ, D_MODEL), 1.0),
        'x_sample': nrm(ks[1], (DEC_BATCH, DEC_SEQ, D_MODEL), 1.0),
        'cache_k': nrm(ks[2], (n_pool, PAGE_SIZE, DEPTH, SB_HEADS, HEAD_DIM), 1.0),
        'cache_v': nrm(ks[3], (n_pool, PAGE_SIZE, DEPTH, SB_HEADS, HEAD_DIM), 1.0),
        'state_ssm': nrm(ks[4], (DEC_BATCH, DEPTH, SSM_HEADS, HEAD_DIM, D_STATE), 0.2),
        'state_conv': nrm(ks[5], (DEC_BATCH, DEPTH, SSM_CONV - 1, CONV_DIM), 1.0),
        'state_ffn_conv': nrm(ks[6], (DEC_BATCH, DEPTH, FFN_CONV - 1, D_FF), 1.0),
        'page_table': page_table,
        'g_mix': gain(ks[8], (DEPTH, D_MODEL)),
        'w_in': nrm(ks[9], (DEPTH, D_MODEL, IN_DIM), D_MODEL ** -0.5),
        'sb_bias': SB_BIAS_INIT + jax.random.uniform(ks[24], (DEPTH, SB_HEADS), f32, -1.0, 1.0),
        'g_sb_out': gain(ks[10], (DEPTH, SB_WIDTH)),
        'w_conv': nrm(ks[11], (DEPTH, SSM_CONV, CONV_DIM), SSM_CONV ** -0.5),
        'b_conv': nrm(ks[12], (DEPTH, CONV_DIM), 0.01),
        'dt_bias': dt_bias,
        'a_log': a_log,
        'd_skip': 1.0 + 0.1 * jax.random.normal(ks[15], (DEPTH, SSM_HEADS), f32),
        'g_ssm': gain(ks[16], (DEPTH, SSM_WIDTH)),
        'w_out': nrm(ks[17], (DEPTH, MIX_WIDTH, D_MODEL), MIX_WIDTH ** -0.5),
        'g_ffn': gain(ks[18], (DEPTH, D_MODEL)),
        'w_up': nrm(ks[19], (DEPTH, D_MODEL, 2 * D_FF), D_MODEL ** -0.5),
        'w_ffn_conv': nrm(ks[20], (DEPTH, FFN_CONV, D_FF), FFN_CONV ** -0.5),
        'b_ffn_conv': nrm(ks[21], (DEPTH, D_FF), 0.01),
        'w_down': nrm(ks[22], (DEPTH, D_FF, D_MODEL), D_FF ** -0.5),
        'g_final': gain(ks[23], (D_MODEL,)),
    }


def reference(x_prompt, x_sample, cache_k, cache_v, state_ssm, state_conv, state_ffn_conv, page_table,
              g_mix, w_in, sb_bias, g_sb_out, w_conv, b_conv, dt_bias, a_log, d_skip, g_ssm, w_out,
              g_ffn, w_up, w_ffn_conv, b_ffn_conv, w_down, g_final):
    xp, xs = x_prompt, x_sample
    new_p = ([], [], [], [], [])
    new_s = ([], [], [], [], [])
    bp = xp.shape[0]
    for l in range(DEPTH):
        lw = dict(g_mix=g_mix[l], w_in=w_in[l], sb_bias=sb_bias[l], g_sb_out=g_sb_out[l], w_conv=w_conv[l],
                  b_conv=b_conv[l], dt_bias=dt_bias[l], a_log=a_log[l], d_skip=d_skip[l],
                  g_ssm=g_ssm[l], w_out=w_out[l], g_ffn=g_ffn[l], w_up=w_up[l],
                  w_ffn_conv=w_ffn_conv[l], b_ffn_conv=b_ffn_conv[l], w_down=w_down[l])
        h0 = jnp.zeros((bp, SSM_HEADS, HEAD_DIM, D_STATE), jnp.float32)
        c0 = jnp.zeros((bp, SSM_CONV - 1, CONV_DIM), xp.dtype)
        f0 = jnp.zeros((bp, FFN_CONV - 1, D_FF), xp.dtype)
        xp, out_p = layer_forward(xp, lw, sb_prompt, h0, c0, f0)
        attend_s = functools.partial(sb_sample, cache_k=cache_k, cache_v=cache_v,
                                     page_table=page_table, layer=l)
        xs, out_s = layer_forward(xs, lw, attend_s, state_ssm[:, l], state_conv[:, l], state_ffn_conv[:, l])
        for lst, val in zip(new_p, out_p):
            lst.append(val)
        for lst, val in zip(new_s, out_s):
            lst.append(val)
    y_prompt = rmsnorm(xp, g_final)
    y_sample = rmsnorm(xs, g_final)
    return (y_prompt, y_sample,
            jnp.stack(new_p[0], axis=2), jnp.stack(new_p[1], axis=2), jnp.stack(new_p[2], axis=1),
            jnp.stack(new_p[3], axis=1), jnp.stack(new_p[4], axis=1),
            jnp.stack(new_s[0], axis=2), jnp.stack(new_s[1], axis=2), jnp.stack(new_s[2], axis=1),
            jnp.stack(new_s[3], axis=1), jnp.stack(new_s[4], axis=1))
```

```python
import functools

import jax
import jax.numpy as jnp
from jax import lax
from jax.experimental import pallas as pl
from jax.experimental.pallas import tpu as pltpu

F32 = jnp.float32
BF16 = jnp.bfloat16

HEAD_DIM = 64
D_STATE = 128
SSM_GROUPS = 4
EPS = 1e-6
LANES = 128
SUBLANES = 8
CHUNK = 128
VMEM_LIMIT = 56 * 1024 * 1024


def _cparams(sem):
    return pltpu.CompilerParams(dimension_semantics=sem, vmem_limit_bytes=VMEM_LIMIT)


def _const_spec(shape):
    nd = len(shape)
    return pl.BlockSpec(shape, lambda *_: (0,) * nd, pipeline_mode=pl.Buffered(1))


def _rms(x, g):
    ms = jnp.mean(x * x, axis=-1, keepdims=True)
    return x * lax.rsqrt(ms + EPS) * g


def _silu(x):
    return x / (1.0 + jnp.exp(-x))


def _dot(a, b):
    return jnp.dot(a, b, preferred_element_type=F32)


def _dot_nt(a, b):
    return lax.dot_general(a, b, (((1,), (1,)), ((), ())), preferred_element_type=F32)


def _split3(x):
    hi = x.astype(BF16)
    r1 = x - hi.astype(F32)
    mid = r1.astype(BF16)
    lo = (r1 - mid.astype(F32)).astype(BF16)
    return hi, mid, lo


def _log_sigmoid_pair(z):
    sp = jnp.log(1.0 + jnp.exp(-jnp.abs(z)))
    lsig = jnp.minimum(z, 0.0) - sp
    return lsig, lsig - z


def _inproj_body(x_ref, g_ref, wqkv_ref, wz_ref, wxbc_ref, wdt_ref, *rest, sbw, scale, transposed_kv):
    h = _rms(x_ref[...], g_ref[...]).astype(BF16)
    qkv = _dot(h, wqkv_ref[...])
    if transposed_kv:
        wkvt_ref, q_ref, v_ref, z_ref, xbc_ref, dt_ref, kt_ref, vt_ref = rest
        kvt = _dot_nt(wkvt_ref[...], h)
        kt_ref[0] = kvt[:sbw]
        vt_ref[0] = kvt[sbw:]
    else:
        q_ref, v_ref, z_ref, xbc_ref, dt_ref, k_ref = rest
        k_ref[...] = qkv[:, sbw:2 * sbw]
    q_ref[...] = qkv[:, :sbw] * scale
    v_ref[...] = qkv[:, 2 * sbw:]
    z_ref[...] = _dot(h, wz_ref[...])
    xbc_ref[...] = _dot(h, wxbc_ref[...])
    dt_ref[...] = _dot(h, wdt_ref[...])


def _inproj(x2d, g, wqkv, wz, wxbc, wdt, wkvt, *, sbw, tm, seq_len=None):
    t, d = x2d.shape
    ssmw, convd = wz.shape[1], wxbc.shape[1]
    row = lambda w: pl.BlockSpec((tm, w), lambda i: (i, 0))
    in_specs = [row(d), _const_spec((1, d)), _const_spec(wqkv.shape), _const_spec(wz.shape),
                _const_spec(wxbc.shape), _const_spec(wdt.shape)]
    args = [x2d, g, wqkv, wz, wxbc, wdt]
    out_specs = [row(sbw), row(sbw), row(ssmw), row(convd), row(LANES)]
    out_shape = [jax.ShapeDtypeStruct((t, w), F32) for w in (sbw, sbw, ssmw, convd, LANES)]
    if seq_len is None:
        out_specs.append(row(sbw))
        out_shape.append(jax.ShapeDtypeStruct((t, sbw), F32))
    else:
        nb = seq_len // tm
        in_specs.append(_const_spec(wkvt.shape))
        args.append(wkvt)
        tr = pl.BlockSpec((1, sbw, tm), lambda i: (i // nb, 0, i % nb))
        out_specs += [tr, tr]
        out_shape += [jax.ShapeDtypeStruct((t // seq_len, sbw, seq_len), F32)] * 2
    return pl.pallas_call(
        functools.partial(_inproj_body, sbw=sbw, scale=HEAD_DIM ** -0.5, transposed_kv=seq_len is not None),
        grid=(t // tm,),
        in_specs=in_specs,
        out_specs=out_specs,
        out_shape=out_shape,
        compiler_params=_cparams(("arbitrary",)),
        name="inproj",
    )(*args)


def _suffix_matrix():
    r = jnp.arange(2 * CHUNK)[:, None] % CHUNK
    c = jnp.arange(2 * CHUNK)[None, :]
    return jnp.where(c < CHUNK, r > c, True).astype(BF16)


def _sb_weights(z, carry, w2, mask):
    lsig, l = _log_sigmoid_pair(z)
    if mask is not None:
        l = jnp.where(mask, l, 0.0)
    hi = l.astype(BF16)
    lo = (l - hi.astype(F32)).astype(BF16)
    r = _dot(jnp.concatenate([hi, lo], axis=1), w2)
    w = jnp.exp(lsig + r[:, :CHUNK] + carry)
    if mask is not None:
        w = jnp.where(mask, w, 0.0)
    return w, carry + r[:, CHUNK:]


def _sb_prompt_body(bias_ref, q_ref, kt_ref, v_ref, w2_ref, o_ref, kb_ref, vb_ref):
    hp = pl.program_id(1)
    i = pl.program_id(2)
    tq = CHUNK

    @pl.when(i == 0)
    def _():
        for j in range(kb_ref.shape[0]):
            kb_ref[j] = kt_ref[0, :, j * tq:(j + 1) * tq].astype(BF16)
        vb_ref[...] = v_ref[0].astype(BF16)

    low = lax.broadcasted_iota(jnp.int32, (tq, LANES), 1) < HEAD_DIM
    q = q_ref[0]
    qh = (jnp.where(low, q, 0.0).astype(BF16), jnp.where(low, 0.0, q).astype(BF16))
    bias = (bias_ref[2 * hp], bias_ref[2 * hp + 1])
    causal = (lax.broadcasted_iota(jnp.int32, (tq, tq), 1)
              < lax.broadcasted_iota(jnp.int32, (tq, tq), 0))
    w2 = w2_ref[...]

    def tile(jb, carry, acc, mask):
        start = pl.multiple_of(jb * tq, tq)
        kblk = kb_ref[jb]
        vblk = vb_ref[pl.ds(start, tq), :]
        new_c, new_a = [], []
        for h in range(2):
            z = _dot(qh[h], kblk) + bias[h]
            w, c = _sb_weights(z, carry[h], w2, mask)
            new_a.append(acc[h] + _dot(w.astype(BF16), vblk))
            new_c.append(c)
        return tuple(new_c), tuple(new_a)

    zero = jnp.zeros((tq, LANES), F32)
    c, a = tile(i, (zero, zero), (zero, zero), causal)

    def body(j, st):
        return tile(i - 1 - j, st[0], st[1], None)

    c, a = lax.fori_loop(0, i, body, (c, a))
    o_ref[0] = jnp.where(low, a[0], a[1])


def _sb_prompt(q, kt, v, bias, w2):
    b, l, sbw = q.shape
    tq = CHUNK
    qspec = pl.BlockSpec((1, tq, LANES), lambda bi, hp, i: (bi, i, hp))
    ktspec = pl.BlockSpec((1, LANES, l), lambda bi, hp, i: (bi, hp, 0))
    vspec = pl.BlockSpec((1, l, LANES), lambda bi, hp, i: (bi, 0, hp))
    return pl.pallas_call(
        _sb_prompt_body,
        grid=(b, sbw // LANES, l // tq),
        in_specs=[pl.BlockSpec(memory_space=pltpu.SMEM), qspec, ktspec, vspec, _const_spec(w2.shape)],
        out_specs=qspec,
        out_shape=jax.ShapeDtypeStruct((b, l, sbw), F32),
        scratch_shapes=[pltpu.VMEM((l // tq, LANES, tq), BF16), pltpu.VMEM((l, LANES), BF16)],
        compiler_params=_cparams(("arbitrary", "arbitrary", "arbitrary")),
        name="sb_prompt",
    )(bias, q, kt, v, w2)


def _sb_sample_body(pt_ref, q_ref, kn_ref, vn_ref, bias_ref, w2_ref, *rest, pages, heads, steps):
    kp = rest[:pages]
    vp = rest[pages:2 * pages]
    o_ref, carry_ref, acc_ref = rest[2 * pages:]
    p = pl.program_id(1)
    nt, width = q_ref.shape[1], q_ref.shape[2]
    rows = nt * heads
    q = q_ref[0]
    qrep = jnp.concatenate([jnp.broadcast_to(q[t:t + 1], (heads, width)) for t in range(nt)], axis=0)
    own = (lax.broadcasted_iota(jnp.int32, (rows, width), 1) // HEAD_DIM
           == lax.broadcasted_iota(jnp.int32, (rows, width), 0) % heads)
    qbd = jnp.where(own, qrep, 0.0)
    bias = bias_ref[...]

    @pl.when(p == 0)
    def _():
        kn = kn_ref[0]
        vn = vn_ref[0]
        t_row = lax.broadcasted_iota(jnp.int32, (rows, 1), 0) // heads
        lsigs, ls = [], []
        for s in range(nt):
            z = jnp.sum(qbd * kn[s:s + 1], axis=1, keepdims=True) + bias[:, :1]
            lsig, l = _log_sigmoid_pair(z)
            lsigs.append(lsig)
            ls.append(jnp.where(s < t_row, l, 0.0))
        o = jnp.zeros((rows, width), F32)
        suffix = jnp.zeros((rows, 1), F32)
        for s in reversed(range(nt)):
            w = jnp.where(s < t_row, jnp.exp(lsigs[s] + suffix), 0.0)
            o = o + w * vn[s:s + 1]
            suffix = suffix + ls[s]
        carry_ref[...] = jnp.broadcast_to(suffix, (rows, CHUNK))
        acc_ref[...] = o

    w2 = w2_ref[...]
    qb = qbd.astype(BF16)
    for g in range(pages):
        z = _dot(qb, kp[g][0].astype(BF16)) + bias
        w, c = _sb_weights(z, carry_ref[...], w2, None)
        carry_ref[...] = c
        acc_ref[...] += _dot_nt(w.astype(BF16), vp[g][0].astype(BF16))

    @pl.when(p == steps - 1)
    def _():
        acc = jnp.where(own, acc_ref[...], 0.0)
        o_ref[0] = jnp.concatenate(
            [jnp.sum(acc[t * heads:(t + 1) * heads], axis=0, keepdims=True) for t in range(nt)], axis=0)


def _sb_sample(q, kn, vn, bias_rows, w2, cache_kt, cache_vt, page_table, *, heads, pages):
    bd, nt, width = q.shape
    rows = nt * heads
    n_pages = page_table.shape[1]
    steps = n_pages // pages
    page = cache_kt.shape[2]

    def page_spec(g):
        def imap(b, p, pt):
            return (pt[b, n_pages - 1 - (p * pages + g)], 0, 0)
        return pl.BlockSpec((1, width, page), imap)

    seq = pl.BlockSpec((1, nt, width), lambda b, p, pt: (b, 0, 0))
    grid_spec = pltpu.PrefetchScalarGridSpec(
        num_scalar_prefetch=1,
        grid=(bd, steps),
        in_specs=[seq, seq, seq, _const_spec(bias_rows.shape), _const_spec(w2.shape)]
        + [page_spec(g) for g in range(pages)] * 2,
        out_specs=seq,
        scratch_shapes=[pltpu.VMEM((rows, CHUNK), F32), pltpu.VMEM((rows, width), F32)],
    )
    return pl.pallas_call(
        functools.partial(_sb_sample_body, pages=pages, heads=heads, steps=steps),
        grid_spec=grid_spec,
        out_shape=jax.ShapeDtypeStruct((bd, nt, width), F32),
        compiler_params=_cparams(("arbitrary", "arbitrary")),
        name="sb_sample",
    )(page_table, q, kn, vn, bias_rows, w2, *([cache_kt] * pages), *([cache_vt] * pages))


def _ssd_body(*refs, nv, has_state, heads, ssmw):
    if has_state:
        (xbc_ref, z_ref, dt_ref, cprev_ref, h0_ref, wconv_ref, bconv_ref, dtb_ref, alog_ref,
         dskip_ref, gssm_ref, ltri_ref, y_ref, conv_ref, ssm_ref, xbuf, ht_ref, zbuf, dtbuf) = refs
    else:
        (xbc_ref, z_ref, dt_ref, wconv_ref, bconv_ref, dtb_ref, alog_ref,
         dskip_ref, gssm_ref, ltri_ref, y_ref, conv_ref, ssm_ref, xbuf, ht_ref) = refs
    c = pl.program_id(1)
    last = pl.num_programs(1) - 1
    cl = CHUNK
    taps = wconv_ref.shape[0]
    convd = xbuf.shape[1]
    gw = D_STATE
    hpg = heads // SSM_GROUPS

    if has_state:
        xbuf[...] = jnp.zeros_like(xbuf)
        xbuf[0:SUBLANES, :] = cprev_ref[0]
        xbuf[SUBLANES:SUBLANES + nv, :] = xbc_ref[0]
        zbuf[...] = jnp.zeros_like(zbuf)
        zbuf[0:nv, :] = z_ref[0]
        dtbuf[...] = jnp.zeros_like(dtbuf)
        dtbuf[0:nv, :] = dt_ref[0]
        zin = zbuf[...]
        dt_raw = dtbuf[...]
        ht_ref[...] = h0_ref[0].T
    else:
        @pl.when(c == 0)
        def _():
            xbuf[0:SUBLANES, :] = jnp.zeros((SUBLANES, convd), F32)
            ht_ref[...] = jnp.zeros_like(ht_ref)
        xbuf[SUBLANES:SUBLANES + cl, :] = xbc_ref[0]
        zin = z_ref[0]
        dt_raw = dt_ref[0]

    xc = bconv_ref[...]
    for tap in range(taps):
        off = SUBLANES - (taps - 1) + tap
        xc = xc + xbuf[off:off + cl, :] * wconv_ref[tap:tap + 1, :]
    xc = _silu(xc)
    xs = xc[:, :ssmw]
    bm = xc[:, ssmw:ssmw + SSM_GROUPS * gw]
    cm = xc[:, ssmw + SSM_GROUPS * gw:]

    @pl.when(c == last)
    def _():
        conv_ref[0] = xbuf[SUBLANES + nv - (taps - 1):SUBLANES + nv, :]

    if not has_state:
        xbuf[0:SUBLANES, :] = xbuf[cl:cl + SUBLANES, :]

    x = dt_raw + dtb_ref[...]
    dt = jnp.maximum(x, 0.0) + jnp.log1p(jnp.exp(-jnp.abs(x)))
    if nv < cl:
        dt = jnp.where(lax.broadcasted_iota(jnp.int32, (cl, LANES), 0) < nv, dt, 0.0)
    a = -jnp.exp(alog_ref[...])
    da = dt * a
    ltri = ltri_ref[...]
    acum = sum(_dot(ltri, part) for part in _split3(da))
    acum_last = acum[cl - 1:cl, :]
    eac = jnp.exp(acum)
    wgt = dt * jnp.exp(acum_last - acum)
    cdec = jnp.exp(acum_last)
    acum_t = acum.T
    dt_t = dt.T
    wgt_t = wgt.T

    causal = (lax.broadcasted_iota(jnp.int32, (cl, cl), 1)
              <= lax.broadcasted_iota(jnp.int32, (cl, cl), 0))
    low = lax.broadcasted_iota(jnp.int32, (cl, LANES), 1) < HEAD_DIM
    low1 = low[:1]
    dskip = dskip_ref[...]

    ys = []
    for g in range(SSM_GROUPS):
        cc = cm[:, g * gw:(g + 1) * gw]
        bc = bm[:, g * gw:(g + 1) * gw]
        cb = _dot_nt(cc.astype(BF16), bc.astype(BF16))
        bc_t = bc.T
        for j in range(hpg // 2):
            h0 = g * hpg + 2 * j
            cols = slice(h0 * HEAD_DIM, (h0 + 2) * HEAD_DIM)
            xs_p = xs[:, cols]
            xs_pb = xs_p.astype(BF16)
            ht_p = ht_ref[:, cols]
            rhs = jnp.concatenate([xs_pb, ht_p.astype(BF16)], axis=0)
            yh, sh = [], []
            for hh in (h0, h0 + 1):
                seg = acum[:, hh:hh + 1] - acum_t[hh:hh + 1, :]
                m = cb * jnp.where(causal, jnp.exp(seg), 0.0) * dt_t[hh:hh + 1, :]
                lhs = jnp.concatenate([m.astype(BF16), (cc * eac[:, hh:hh + 1]).astype(BF16)], axis=1)
                yh.append(_dot(lhs, rhs))
                sh.append(_dot((bc_t * wgt_t[hh:hh + 1, :]).astype(BF16), xs_pb))
            cd = jnp.where(low1, cdec[:, h0:h0 + 1], cdec[:, h0 + 1:h0 + 2])
            ht_ref[:, cols] = cd * ht_p + jnp.where(low, sh[0], sh[1])
            ys.append(jnp.where(low, yh[0], yh[1]) + dskip[:, cols] * xs_p)
    y = jnp.concatenate(ys, axis=1) * _silu(zin)

    gsz = ssmw // SSM_GROUPS
    outs = []
    for g in range(SSM_GROUPS):
        outs.append(_rms(y[:, g * gsz:(g + 1) * gsz], gssm_ref[:, g * gsz:(g + 1) * gsz]))
    y = jnp.concatenate(outs, axis=1)
    y_ref[0] = y[:nv] if nv < cl else y

    @pl.when(c == last)
    def _():
        ssm_ref[0] = ht_ref[...].T


def _ssd(xbc, z, dt, params, ltri, *, heads, state=None):
    b, l, convd = xbc.shape
    ssmw = z.shape[2]
    has_state = state is not None
    taps = params[0].shape[0]
    if has_state:
        nv, nchunks = l, 1
    else:
        nv, nchunks = CHUNK, l // CHUNK
    blk = lambda w: pl.BlockSpec((1, nv, w), lambda bi, c: (bi, c, 0))
    per_seq = lambda r, w: pl.BlockSpec((1, r, w), lambda bi, c: (bi, 0, 0))
    const = lambda a: _const_spec(a.shape)
    in_specs = [blk(convd), blk(ssmw), blk(LANES)]
    args = [xbc, z, dt]
    scratch = [pltpu.VMEM((CHUNK + SUBLANES, convd), F32), pltpu.VMEM((D_STATE, ssmw), F32)]
    if has_state:
        in_specs += [per_seq(SUBLANES, convd), per_seq(ssmw, D_STATE)]
        args += list(state)
        scratch += [pltpu.VMEM((CHUNK, ssmw), F32), pltpu.VMEM((CHUNK, LANES), F32)]
    in_specs += [const(p) for p in params] + [const(ltri)]
    args += list(params) + [ltri]
    return pl.pallas_call(
        functools.partial(_ssd_body, nv=nv, has_state=has_state, heads=heads, ssmw=ssmw),
        grid=(b, nchunks),
        in_specs=in_specs,
        out_specs=[blk(ssmw), per_seq(taps - 1, convd), per_seq(ssmw, D_STATE)],
        out_shape=[jax.ShapeDtypeStruct((b, l, ssmw), F32),
                   jax.ShapeDtypeStruct((b, taps - 1, convd), F32),
                   jax.ShapeDtypeStruct((b, ssmw, D_STATE), F32)],
        scratch_shapes=scratch,
        compiler_params=_cparams(("arbitrary", "arbitrary")),
        name="ssd_sample" if has_state else "ssd_prompt",
    )(*args)


def _outproj_body(o_ref, y_ref, x_ref, gsb_ref, wsb_ref, wy_ref, x1_ref):
    o = _rms(o_ref[...], gsb_ref[...]).astype(BF16)
    x1_ref[...] = x_ref[...] + _dot(o, wsb_ref[...]) + _dot(y_ref[...].astype(BF16), wy_ref[...])


def _outproj(o_sb, y, x2d, gsb, wsb, wy, *, tm):
    t, d = x2d.shape
    row = lambda w: pl.BlockSpec((tm, w), lambda i: (i, 0))
    return pl.pallas_call(
        _outproj_body,
        grid=(t // tm,),
        in_specs=[row(o_sb.shape[1]), row(y.shape[1]), row(d), _const_spec(gsb.shape),
                  _const_spec(wsb.shape), _const_spec(wy.shape)],
        out_specs=row(d),
        out_shape=jax.ShapeDtypeStruct((t, d), F32),
        compiler_params=_cparams(("arbitrary",)),
        name="outproj",
    )(o_sb, y, x2d, gsb, wsb, wy)


def _ffn_tail(x1, gate, g1, g2, val, wc_ref, bc_ref, wdown_ref, gfin_ref):
    gc = bc_ref[...] + wc_ref[2:3, :] * gate + wc_ref[1:2, :] * g1 + wc_ref[0:1, :] * g2
    u = (_silu(gc) * val).astype(BF16)
    return _rms(x1 + _dot(u, wdown_ref[...]), gfin_ref[...])


def _ffn_prompt_body(x1_ref, gffn_ref, wup_ref, wc_ref, bc_ref, wdown_ref, gfin_ref,
                     y_ref, fnew_ref, gbuf, *, tm, dff):
    j = pl.program_id(1)

    @pl.when(j == 0)
    def _():
        gbuf[0:SUBLANES, :] = jnp.zeros((SUBLANES, dff), F32)

    x1 = x1_ref[0]
    up = _dot(_rms(x1, gffn_ref[...]).astype(BF16), wup_ref[...])
    gate = up[:, :dff]
    gbuf[SUBLANES:SUBLANES + tm, :] = gate
    g1 = gbuf[SUBLANES - 1:SUBLANES - 1 + tm, :]
    g2 = gbuf[SUBLANES - 2:SUBLANES - 2 + tm, :]
    y_ref[0] = _ffn_tail(x1, gate, g1, g2, up[:, dff:], wc_ref, bc_ref, wdown_ref, gfin_ref)
    gbuf[0:SUBLANES, :] = gbuf[tm:tm + SUBLANES, :]

    @pl.when(j == pl.num_programs(1) - 1)
    def _():
        fnew_ref[0] = gbuf[SUBLANES - 2:SUBLANES, :]


def _ffn_prompt(x1, gffn, wup, wc, bc, wdown, gfin, *, tm):
    b, l, d = x1.shape
    dff = wdown.shape[0]
    blk = pl.BlockSpec((1, tm, d), lambda bi, j: (bi, j, 0))
    const = lambda a: _const_spec(a.shape)
    return pl.pallas_call(
        functools.partial(_ffn_prompt_body, tm=tm, dff=dff),
        grid=(b, l // tm),
        in_specs=[blk] + [const(a) for a in (gffn, wup, wc, bc, wdown, gfin)],
        out_specs=[blk, pl.BlockSpec((1, 2, dff), lambda bi, j: (bi, 0, 0))],
        out_shape=[jax.ShapeDtypeStruct((b, l, d), F32), jax.ShapeDtypeStruct((b, 2, dff), F32)],
        scratch_shapes=[pltpu.VMEM((tm + SUBLANES, dff), F32)],
        compiler_params=_cparams(("arbitrary", "arbitrary")),
        name="ffn_prompt",
    )(x1, gffn, wup, wc, bc, wdown, gfin)


def _ffn_sample_body(x1_ref, p1_ref, p2_ref, gffn_ref, wup_ref, wc_ref, bc_ref, wdown_ref, gfin_ref,
                     y_ref, gate_ref, *, seq, dff):
    x1 = x1_ref[...]
    up = _dot(_rms(x1, gffn_ref[...]).astype(BF16), wup_ref[...])
    gate = up[:, :dff]
    gate_ref[...] = gate
    pos = lax.broadcasted_iota(jnp.int32, (x1.shape[0], 1), 0) % seq
    g1 = jnp.where(pos >= 1, pltpu.roll(gate, 1, axis=0), p1_ref[...])
    g2 = jnp.where(pos >= 2, pltpu.roll(gate, 2, axis=0), p2_ref[...])
    y_ref[...] = _ffn_tail(x1, gate, g1, g2, up[:, dff:], wc_ref, bc_ref, wdown_ref, gfin_ref)


def _ffn_sample(x1, p1, p2, gffn, wup, wc, bc, wdown, gfin, *, seq, tm):
    t, d = x1.shape
    dff = wdown.shape[0]
    row = lambda w: pl.BlockSpec((tm, w), lambda i: (i, 0))
    return pl.pallas_call(
        functools.partial(_ffn_sample_body, seq=seq, dff=dff),
        grid=(t // tm,),
        in_specs=[row(d), row(dff), row(dff)] + [_const_spec(a.shape) for a in (gffn, wup, wc, bc, wdown, gfin)],
        out_specs=[row(d), row(dff)],
        out_shape=[jax.ShapeDtypeStruct((t, d), F32), jax.ShapeDtypeStruct((t, dff), F32)],
        compiler_params=_cparams(("arbitrary",)),
        name="ffn_sample",
    )(x1, p1, p2, gffn, wup, wc, bc, wdown, gfin)


def _pad_lanes(a):
    return jnp.pad(a, ((0, 0), (0, LANES - a.shape[1])))


def kernel(x_prompt, x_sample, cache_k, cache_v, state_ssm, state_conv, state_ffn_conv, page_table,
           g_mix, w_in, sb_bias, g_sb_out, w_conv, b_conv, dt_bias, a_log, d_skip, g_ssm, w_out,
           g_ffn, w_up, w_ffn_conv, b_ffn_conv, w_down, g_final):
    assert g_mix.shape[0] == 1, "single-layer step"
    bp, lp, d = x_prompt.shape
    bd, ld, _ = x_sample.shape
    n_pool, page, _, sb_heads, _ = cache_k.shape
    n_pages = page_table.shape[1]
    assert page == CHUNK
    sbw = sb_heads * HEAD_DIM
    heads = a_log.shape[1]
    ssmw = heads * HEAD_DIM
    convd = w_conv.shape[2]
    dff = w_down.shape[1]

    w = w_in[0]
    wqkv = w[:, :3 * sbw].astype(BF16)
    wz = w[:, 3 * sbw:3 * sbw + ssmw].astype(BF16)
    wxbc = w[:, 3 * sbw + ssmw:3 * sbw + ssmw + convd].astype(BF16)
    wdt = _pad_lanes(w[:, 3 * sbw + ssmw + convd:]).astype(BF16)
    wsb = w_out[0, :sbw].astype(BF16)
    wy = w_out[0, sbw:].astype(BF16)
    wup = w_up[0].astype(BF16)
    wdown = w_down[0].astype(BF16)
    ssd_params = (w_conv[0], b_conv, _pad_lanes(dt_bias), _pad_lanes(a_log),
                  jnp.repeat(d_skip, HEAD_DIM, axis=1), g_ssm)
    ltri = jnp.tril(jnp.ones((CHUNK, CHUNK), BF16))
    w2 = _suffix_matrix()

    wkvt = w[:, sbw:3 * sbw].T.astype(BF16)

    def mix_in(x2d, tm, seq_len=None):
        return _inproj(x2d, g_mix, wqkv, wz, wxbc, wdt, wkvt, sbw=sbw, tm=min(tm, x2d.shape[0]),
                       seq_len=seq_len)

    xp2 = x_prompt.reshape(bp * lp, d)
    q, v_tok, z, xbc, dt, kt_p, vt_p = mix_in(xp2, 256, seq_len=lp)
    to_seq = lambda a: a.reshape(bp, lp, a.shape[-1])
    o_sb = _sb_prompt(to_seq(q), kt_p, to_seq(v_tok), sb_bias[0], w2)
    y, conv_p, ssm_p = _ssd(to_seq(xbc), to_seq(z), to_seq(dt), ssd_params, ltri, heads=heads)
    x1 = _outproj(o_sb.reshape(bp * lp, sbw), y.reshape(bp * lp, ssmw), xp2, g_sb_out, wsb, wy, tm=512)
    y_prompt, ffn_p = _ffn_prompt(x1.reshape(bp, lp, d), g_ffn, wup, w_ffn_conv[0], b_ffn_conv, wdown,
                                  g_final[None], tm=256)

    ts = bd * ld
    xs2 = x_sample.reshape(ts, d)
    q, v_s, z, xbc, dt, k_s = mix_in(xs2, 256)
    to_seq = lambda a: a.reshape(bd, ld, a.shape[-1])
    bias_rows = jnp.broadcast_to(jnp.tile(sb_bias[0], ld)[:, None], (ld * sb_heads, CHUNK))
    pool_t = lambda c: jnp.transpose(c[:, :, 0], (0, 2, 3, 1)).reshape(n_pool, sbw, page)
    o_rows = _sb_sample(to_seq(q), to_seq(k_s), to_seq(v_s), bias_rows, w2, pool_t(cache_k), pool_t(cache_v),
                        page_table, heads=sb_heads, pages=max(g for g in (1, 2, 4, 8) if n_pages % g == 0))
    y, conv_s, ssm_s = _ssd(to_seq(xbc), to_seq(z), to_seq(dt), ssd_params, ltri, heads=heads,
                            state=(jnp.pad(state_conv[:, 0], ((0, 0), (SUBLANES - state_conv.shape[2], 0), (0, 0))),
                                   state_ssm.reshape(bd, ssmw, D_STATE)))
    x1 = _outproj(o_rows.reshape(ts, sbw), y.reshape(ts, ssmw), xs2, g_sb_out, wsb, wy, tm=ts)
    prev = state_ffn_conv[:, 0]
    p1 = jnp.pad(prev[:, 1:2], ((0, 0), (0, ld - 1), (0, 0))).reshape(ts, dff)
    p2 = jnp.pad(prev, ((0, 0), (0, ld - 2), (0, 0))).reshape(ts, dff)
    y_sample, gate = _ffn_sample(x1, p1, p2, g_ffn, wup, w_ffn_conv[0], b_ffn_conv, wdown, g_final[None],
                                 seq=ld, tm=ts)
    ffn_s = gate.reshape(bd, ld, dff)[:, ld - 2:]

    kv_p = lambda a: jnp.transpose(a.reshape(bp, 1, sb_heads, HEAD_DIM, lp), (0, 4, 1, 2, 3))
    kv_s = lambda a: a.reshape(bd, ld, 1, sb_heads, HEAD_DIM)
    return (y_prompt, y_sample.reshape(bd, ld, d),
            kv_p(kt_p), kv_p(vt_p), ssm_p.reshape(bp, 1, heads, HEAD_DIM, D_STATE),
            conv_p[:, None], ffn_p[:, None],
            kv_s(k_s), kv_s(v_s), ssm_s.reshape(bd, 1, heads, HEAD_DIM, D_STATE),
            conv_s[:, None], ffn_s[:, None])
```

```python
import functools

import jax
import jax.numpy as jnp
from jax import lax
from jax.experimental import pallas as pl
from jax.experimental.pallas import tpu as pltpu

F32 = jnp.float32
BF16 = jnp.bfloat16

HEAD_DIM = 64
D_STATE = 128
SSM_GROUPS = 4
EPS = 1e-6
LOG2E = 1.4426950408889634
LANES = 128
SUBLANES = 8
CHUNK = 128
VMEM_LIMIT = 56 * 1024 * 1024


def _cparams(sem):
    return pltpu.CompilerParams(dimension_semantics=sem, vmem_limit_bytes=VMEM_LIMIT)


def _const_spec(shape):
    nd = len(shape)
    return pl.BlockSpec(shape, lambda *_: (0,) * nd, pipeline_mode=pl.Buffered(1))


def _rms(x, g):
    ms = jnp.mean(x * x, axis=-1, keepdims=True)
    return x * lax.rsqrt(ms + EPS) * g


def _silu(x):
    return x / (1.0 + jnp.exp(-x))


def _dot(a, b):
    return jnp.dot(a, b, preferred_element_type=F32)


def _dot_nt(a, b):
    return lax.dot_general(a, b, (((1,), (1,)), ((), ())), preferred_element_type=F32)


def _split3(x):
    hi = x.astype(BF16)
    r1 = x - hi.astype(F32)
    mid = r1.astype(BF16)
    lo = (r1 - mid.astype(F32)).astype(BF16)
    return hi, mid, lo


def _inproj_body(x_ref, g_ref, wqkv_ref, wz_ref, wxbc_ref, wdt_ref, *rest, sbw, scale, transposed_kv):
    h = _rms(x_ref[...], g_ref[...]).astype(BF16)
    qkv = _dot(h, wqkv_ref[...])
    if transposed_kv:
        wkvt_ref, q_ref, v_ref, z_ref, xbc_ref, dt_ref, kt_ref, vt_ref = rest
        kvt = _dot_nt(wkvt_ref[...], h)
        kt_ref[0] = kvt[:sbw]
        vt_ref[0] = kvt[sbw:]
    else:
        q_ref, v_ref, z_ref, xbc_ref, dt_ref, k_ref = rest
        k_ref[...] = qkv[:, sbw:2 * sbw]
    q_ref[...] = qkv[:, :sbw] * scale
    v_ref[...] = qkv[:, 2 * sbw:]
    z_ref[...] = _dot(h, wz_ref[...])
    xbc_ref[...] = _dot(h, wxbc_ref[...])
    dt_ref[...] = _dot(h, wdt_ref[...])


def _inproj(x2d, g, wqkv, wz, wxbc, wdt, wkvt, *, sbw, tm, seq_len=None):
    t, d = x2d.shape
    ssmw, convd = wz.shape[1], wxbc.shape[1]
    row = lambda w: pl.BlockSpec((tm, w), lambda i: (i, 0))
    in_specs = [row(d), _const_spec((1, d)), _const_spec(wqkv.shape), _const_spec(wz.shape),
                _const_spec(wxbc.shape), _const_spec(wdt.shape)]
    args = [x2d, g, wqkv, wz, wxbc, wdt]
    out_specs = [row(sbw), row(sbw), row(ssmw), row(convd), row(LANES)]
    out_shape = [jax.ShapeDtypeStruct((t, w), F32) for w in (sbw, sbw, ssmw, convd, LANES)]
    if seq_len is None:
        out_specs.append(row(sbw))
        out_shape.append(jax.ShapeDtypeStruct((t, sbw), F32))
    else:
        nb = seq_len // tm
        in_specs.append(_const_spec(wkvt.shape))
        args.append(wkvt)
        tr = pl.BlockSpec((1, sbw, tm), lambda i: (i // nb, 0, i % nb))
        out_specs += [tr, tr]
        out_shape += [jax.ShapeDtypeStruct((t // seq_len, sbw, seq_len), F32)] * 2
    return pl.pallas_call(
        functools.partial(_inproj_body, sbw=sbw, scale=HEAD_DIM ** -0.5 * LOG2E, transposed_kv=seq_len is not None),
        grid=(t // tm,),
        in_specs=in_specs,
        out_specs=out_specs,
        out_shape=out_shape,
        compiler_params=_cparams(("arbitrary",)),
        name="inproj",
    )(*args)


def _suffix_matrix():
    r = jnp.arange(2 * CHUNK)[:, None] % CHUNK
    c = jnp.arange(2 * CHUNK)[None, :]
    return jnp.where(c < CHUNK, r >= c, True).astype(BF16)


def _log2_one_minus_sigmoid(z2):
    sign = jnp.uint32(0x80000000)
    neg_abs = lax.bitcast_convert_type(lax.bitcast_convert_type(z2, jnp.uint32) | sign, F32)
    return jnp.log(1.0 + jnp.exp2(neg_abs)) * (-LOG2E) - jnp.maximum(z2, 0.0)


def _split2(x):
    hi = x.astype(BF16)
    return hi, (x - hi.astype(F32)).astype(BF16)


def _sb_prompt_body(q_ref, kt_ref, v_ref, btile_ref, w2_ref, o_ref, kbd_ref, vbd_ref, carry_ref, acc_ref):
    i = pl.program_id(1)
    tq = CHUNK
    npairs, nblk = kbd_ref.shape[0], kbd_ref.shape[1]
    low = lax.broadcasted_iota(jnp.int32, (tq, LANES), 1) < HEAD_DIM

    @pl.when(i == 0)
    def _():
        zero_k = jnp.zeros((HEAD_DIM, tq), BF16)
        for pr in range(npairs):
            rows = slice(pr * LANES, (pr + 1) * LANES)
            for j in range(nblk):
                keys = slice(j * tq, (j + 1) * tq)
                k2 = kt_ref[0, rows, keys].astype(BF16)
                kbd_ref[pr, j, 0:HEAD_DIM, :] = jnp.concatenate([k2[:HEAD_DIM], zero_k], axis=1)
                kbd_ref[pr, j, HEAD_DIM:LANES, :] = jnp.concatenate([zero_k, k2[HEAD_DIM:]], axis=1)
                kbd_ref[pr, j, LANES:, :] = btile_ref[pr]
                v2 = v_ref[0, keys, rows]
                vbd_ref[pr, j, 0:tq, :] = jnp.where(low, v2, 0.0).astype(BF16)
                vbd_ref[pr, j, tq:, :] = jnp.where(low, 0.0, v2).astype(BF16)

    lane = lax.broadcasted_iota(jnp.int32, (tq, LANES), 1)
    bias_cols = jnp.where(lane < 3, 1.0, 0.0)
    qx = [jnp.concatenate([q_ref[0, :, pr * LANES:(pr + 1) * LANES], bias_cols], axis=1).astype(BF16)
          for pr in range(npairs)]
    causal = (lax.broadcasted_iota(jnp.int32, (tq, tq), 1) < lax.broadcasted_iota(jnp.int32, (tq, tq), 0))
    causal2 = jnp.concatenate([causal, causal], axis=1)
    w2 = w2_ref[...]

    def tile(jbs, first):
        zs, parts = [], []
        for pr in range(npairs):
            for jb in jbs:
                zz = _dot(qx[pr], kbd_ref[pr, jb])
                l = _log2_one_minus_sigmoid(zz)
                if first:
                    l = jnp.where(causal2, l, 0.0)
                hi, lo = _split2(l)
                for h in range(2):
                    cols = slice(h * tq, (h + 1) * tq)
                    zs.append(zz[:, cols])
                    parts.append(jnp.concatenate([hi[:, cols], lo[:, cols]], axis=1))
        r = _dot(jnp.concatenate(parts, axis=0), w2)
        per_pair = 2 * len(jbs)
        for pr in range(npairs):
            ws = []
            cs = [None, None] if first else [carry_ref[2 * pr], carry_ref[2 * pr + 1]]
            for n in range(pr * per_pair, (pr + 1) * per_pair):
                h = n % 2
                rr = r[n * tq:(n + 1) * tq]
                suf, tot = rr[:, :tq], rr[:, tq:]
                if first:
                    ws.append(jnp.where(causal, jnp.exp2(zs[n] + suf), 0.0))
                    cs[h] = tot
                else:
                    ws.append(jnp.exp2(zs[n] + suf + cs[h]))
                    cs[h] = cs[h] + tot
            carry_ref[2 * pr] = cs[0]
            carry_ref[2 * pr + 1] = cs[1]
            lhs = jnp.concatenate(ws, axis=1).astype(BF16)
            rhs = jnp.concatenate([vbd_ref[pr, jb] for jb in jbs], axis=0)
            if first:
                acc_ref[pr] = _dot(lhs, rhs)
            else:
                acc_ref[pr] += _dot(lhs, rhs)

    tile([i], True)
    odd = i % 2

    @pl.when(odd == 1)
    def _():
        tile([i - 1], False)

    def body(j, carry):
        top = i - odd - 1 - 2 * j
        tile([top, top - 1], False)
        return carry

    lax.fori_loop(0, i // 2, body, 0)
    o_ref[0] = jnp.concatenate([acc_ref[pr] for pr in range(npairs)], axis=1)


def _sb_prompt(q, kt, v, btile, w2):
    b, l, sbw = q.shape
    tq = CHUNK
    npairs = sbw // LANES
    qspec = pl.BlockSpec((1, tq, sbw), lambda bi, i: (bi, i, 0))
    return pl.pallas_call(
        _sb_prompt_body,
        grid=(b, l // tq),
        in_specs=[qspec, pl.BlockSpec((1, sbw, l), lambda bi, i: (bi, 0, 0)),
                  pl.BlockSpec((1, l, sbw), lambda bi, i: (bi, 0, 0)),
                  _const_spec(btile.shape), _const_spec(w2.shape)],
        out_specs=qspec,
        out_shape=jax.ShapeDtypeStruct((b, l, sbw), F32),
        scratch_shapes=[pltpu.VMEM((npairs, l // tq, 2 * LANES, 2 * tq), BF16),
                        pltpu.VMEM((npairs, l // tq, 2 * tq, LANES), BF16),
                        pltpu.VMEM((2 * npairs, tq, LANES), F32),
                        pltpu.VMEM((npairs, tq, LANES), F32)],
        compiler_params=_cparams(("arbitrary", "arbitrary")),
        name="sb_prompt",
    )(q, kt, v, btile, w2)


def _sb_sample_body(pt_ref, q_ref, kn_ref, vn_ref, bias_ref, w2_ref, *rest, pages, heads, steps):
    kp = rest[:pages]
    vp = rest[pages:2 * pages]
    o_ref, carry_ref, acc_ref = rest[2 * pages:]
    p = pl.program_id(1)
    nt, width = q_ref.shape[1], q_ref.shape[2]
    rows = nt * heads
    q = q_ref[0]
    qrep = jnp.concatenate([jnp.broadcast_to(q[t:t + 1], (heads, width)) for t in range(nt)], axis=0)
    own = (lax.broadcasted_iota(jnp.int32, (rows, width), 1) // HEAD_DIM
           == lax.broadcasted_iota(jnp.int32, (rows, width), 0) % heads)
    qbd = jnp.where(own, qrep, 0.0)
    bias = bias_ref[...]

    @pl.when(p == 0)
    def _():
        kn = kn_ref[0]
        vn = vn_ref[0]
        t_row = lax.broadcasted_iota(jnp.int32, (rows, 1), 0) // heads
        zs, ls = [], []
        for s in range(nt):
            z2 = jnp.sum(qbd * kn[s:s + 1], axis=1, keepdims=True) + bias[:, :1]
            zs.append(z2)
            ls.append(jnp.where(s < t_row, _log2_one_minus_sigmoid(z2), 0.0))
        o = jnp.zeros((rows, width), F32)
        suffix = jnp.zeros((rows, 1), F32)
        for s in reversed(range(nt)):
            w = jnp.where(s < t_row, jnp.exp2(zs[s] + ls[s] + suffix), 0.0)
            o = o + w * vn[s:s + 1]
            suffix = suffix + ls[s]
        carry_ref[...] = jnp.broadcast_to(suffix, (rows, CHUNK))
        acc_ref[...] = o

    w2 = w2_ref[...]
    qb = qbd.astype(BF16)
    carry = carry_ref[...]
    acc = acc_ref[...]
    zs, parts = [], []
    for g in range(0, pages, 2):
        kk = jnp.concatenate([kp[g][0], kp[g + 1][0]], axis=1).astype(BF16)
        zz = _dot(qb, kk) + bias
        hi, lo = _split2(_log2_one_minus_sigmoid(zz))
        for h in range(2):
            cols = slice(h * CHUNK, (h + 1) * CHUNK)
            zs.append(zz[:, cols])
            parts.append(jnp.concatenate([hi[:, cols], lo[:, cols]], axis=1))
    r = _dot(jnp.concatenate(parts, axis=0), w2)
    ws = []
    for g in range(pages):
        rr = r[g * rows:(g + 1) * rows]
        ws.append(jnp.exp2(zs[g] + rr[:, :CHUNK] + carry))
        carry = carry + rr[:, CHUNK:]
    for g in range(0, pages, 2):
        vv = jnp.concatenate([vp[g][0], vp[g + 1][0]], axis=1).astype(BF16)
        acc = acc + _dot_nt(jnp.concatenate(ws[g:g + 2], axis=1).astype(BF16), vv)
    carry_ref[...] = carry
    acc_ref[...] = acc

    @pl.when(p == steps - 1)
    def _():
        mine = jnp.where(own, acc, 0.0)
        o_ref[0] = jnp.concatenate(
            [jnp.sum(mine[t * heads:(t + 1) * heads], axis=0, keepdims=True) for t in range(nt)], axis=0)


def _sb_sample(q, kn, vn, bias_rows, w2, cache_kt, cache_vt, page_table, *, heads, pages):
    bd, nt, width = q.shape
    rows = nt * heads
    n_pages = page_table.shape[1]
    steps = n_pages // pages
    page = cache_kt.shape[2]

    def page_spec(g):
        def imap(b, p, pt):
            return (pt[b, n_pages - 1 - (p * pages + g)], 0, 0)
        return pl.BlockSpec((1, width, page), imap)

    seq = pl.BlockSpec((1, nt, width), lambda b, p, pt: (b, 0, 0))
    grid_spec = pltpu.PrefetchScalarGridSpec(
        num_scalar_prefetch=1,
        grid=(bd, steps),
        in_specs=[seq, seq, seq, _const_spec(bias_rows.shape), _const_spec(w2.shape)]
        + [page_spec(g) for g in range(pages)] * 2,
        out_specs=seq,
        scratch_shapes=[pltpu.VMEM((rows, CHUNK), F32), pltpu.VMEM((rows, width), F32)],
    )
    return pl.pallas_call(
        functools.partial(_sb_sample_body, pages=pages, heads=heads, steps=steps),
        grid_spec=grid_spec,
        out_shape=jax.ShapeDtypeStruct((bd, nt, width), F32),
        compiler_params=_cparams(("arbitrary", "arbitrary")),
        name="sb_sample",
    )(page_table, q, kn, vn, bias_rows, w2, *([cache_kt] * pages), *([cache_vt] * pages))


def _ssd_body(*refs, nv, has_state, heads, ssmw):
    if has_state:
        (xbc_ref, z_ref, dt_ref, cprev_ref, h0_ref, wconv_ref, bconv_ref, dtb_ref, alog_ref,
         dskip_ref, gssm_ref, ltri_ref, y_ref, conv_ref, ssm_ref, xbuf, ht_ref, zbuf, dtbuf) = refs
    else:
        (xbc_ref, z_ref, dt_ref, wconv_ref, bconv_ref, dtb_ref, alog_ref,
         dskip_ref, gssm_ref, ltri_ref, y_ref, conv_ref, ssm_ref, xbuf, ht_ref) = refs
    c = pl.program_id(1)
    last = pl.num_programs(1) - 1
    cl = CHUNK
    taps = wconv_ref.shape[0]
    convd = xbuf.shape[1]
    gw = D_STATE
    hpg = heads // SSM_GROUPS

    if has_state:
        xbuf[...] = jnp.zeros_like(xbuf)
        xbuf[0:SUBLANES, :] = cprev_ref[0]
        xbuf[SUBLANES:SUBLANES + nv, :] = xbc_ref[0]
        zbuf[...] = jnp.zeros_like(zbuf)
        zbuf[0:nv, :] = z_ref[0]
        dtbuf[...] = jnp.zeros_like(dtbuf)
        dtbuf[0:nv, :] = dt_ref[0]
        zin = zbuf[...]
        dt_raw = dtbuf[...]
        ht_ref[...] = h0_ref[0].T
    else:
        @pl.when(c == 0)
        def _():
            xbuf[0:SUBLANES, :] = jnp.zeros((SUBLANES, convd), F32)
            ht_ref[...] = jnp.zeros_like(ht_ref)
        xbuf[SUBLANES:SUBLANES + cl, :] = xbc_ref[0]
        zin = z_ref[0]
        dt_raw = dt_ref[0]

    xc = bconv_ref[...]
    for tap in range(taps):
        off = SUBLANES - (taps - 1) + tap
        xc = xc + xbuf[off:off + cl, :] * wconv_ref[tap:tap + 1, :]
    xc = _silu(xc)
    xs = xc[:, :ssmw]
    bm = xc[:, ssmw:ssmw + SSM_GROUPS * gw]
    cm = xc[:, ssmw + SSM_GROUPS * gw:]

    @pl.when(c == last)
    def _():
        conv_ref[0] = xbuf[SUBLANES + nv - (taps - 1):SUBLANES + nv, :]

    if not has_state:
        xbuf[0:SUBLANES, :] = xbuf[cl:cl + SUBLANES, :]

    x = dt_raw + dtb_ref[...]
    dt = jnp.maximum(x, 0.0) + jnp.log1p(jnp.exp(-jnp.abs(x)))
    if nv < cl:
        dt = jnp.where(lax.broadcasted_iota(jnp.int32, (cl, LANES), 0) < nv, dt, 0.0)
    a = -jnp.exp(alog_ref[...])
    da = dt * a
    ltri = ltri_ref[...]
    acum = sum(_dot(ltri, part) for part in _split3(da))
    acum_last = acum[cl - 1:cl, :]
    eac = jnp.exp(acum)
    wgt = dt * jnp.exp(acum_last - acum)
    cdec = jnp.exp(acum_last)
    acum_t = acum.T
    dt_t = dt.T
    wgt_t = wgt.T

    causal = (lax.broadcasted_iota(jnp.int32, (cl, cl), 1)
              <= lax.broadcasted_iota(jnp.int32, (cl, cl), 0))
    low = lax.broadcasted_iota(jnp.int32, (cl, LANES), 1) < HEAD_DIM
    low1 = low[:1]
    dskip = dskip_ref[...]

    ys = []
    for g in range(SSM_GROUPS):
        cc = cm[:, g * gw:(g + 1) * gw]
        bc = bm[:, g * gw:(g + 1) * gw]
        cb = _dot_nt(cc.astype(BF16), bc.astype(BF16))
        bc_t = bc.T
        for j in range(hpg // 2):
            h0 = g * hpg + 2 * j
            cols = slice(h0 * HEAD_DIM, (h0 + 2) * HEAD_DIM)
            xs_p = xs[:, cols]
            xs_pb = xs_p.astype(BF16)
            ht_p = ht_ref[:, cols]
            rhs = jnp.concatenate([xs_pb, ht_p.astype(BF16)], axis=0)
            yh, sh = [], []
            for hh in (h0, h0 + 1):
                seg = acum[:, hh:hh + 1] - acum_t[hh:hh + 1, :]
                m = cb * jnp.where(causal, jnp.exp(seg), 0.0) * dt_t[hh:hh + 1, :]
                lhs = jnp.concatenate([m.astype(BF16), (cc * eac[:, hh:hh + 1]).astype(BF16)], axis=1)
                yh.append(_dot(lhs, rhs))
                sh.append(_dot((bc_t * wgt_t[hh:hh + 1, :]).astype(BF16), xs_pb))
            cd = jnp.where(low1, cdec[:, h0:h0 + 1], cdec[:, h0 + 1:h0 + 2])
            ht_ref[:, cols] = cd * ht_p + jnp.where(low, sh[0], sh[1])
            ys.append(jnp.where(low, yh[0], yh[1]) + dskip[:, cols] * xs_p)
    y = jnp.concatenate(ys, axis=1) * _silu(zin)

    gsz = ssmw // SSM_GROUPS
    outs = []
    for g in range(SSM_GROUPS):
        outs.append(_rms(y[:, g * gsz:(g + 1) * gsz], gssm_ref[:, g * gsz:(g + 1) * gsz]))
    y = jnp.concatenate(outs, axis=1)
    y_ref[0] = y[:nv] if nv < cl else y

    @pl.when(c == last)
    def _():
        ssm_ref[0] = ht_ref[...].T


def _ssd(xbc, z, dt, params, ltri, *, heads, state=None):
    b, l, convd = xbc.shape
    ssmw = z.shape[2]
    has_state = state is not None
    taps = params[0].shape[0]
    if has_state:
        nv, nchunks = l, 1
    else:
        nv, nchunks = CHUNK, l // CHUNK
    blk = lambda w: pl.BlockSpec((1, nv, w), lambda bi, c: (bi, c, 0))
    per_seq = lambda r, w: pl.BlockSpec((1, r, w), lambda bi, c: (bi, 0, 0))
    const = lambda a: _const_spec(a.shape)
    in_specs = [blk(convd), blk(ssmw), blk(LANES)]
    args = [xbc, z, dt]
    scratch = [pltpu.VMEM((CHUNK + SUBLANES, convd), F32), pltpu.VMEM((D_STATE, ssmw), F32)]
    if has_state:
        in_specs += [per_seq(SUBLANES, convd), per_seq(ssmw, D_STATE)]
        args += list(state)
        scratch += [pltpu.VMEM((CHUNK, ssmw), F32), pltpu.VMEM((CHUNK, LANES), F32)]
    in_specs += [const(p) for p in params] + [const(ltri)]
    args += list(params) + [ltri]
    return pl.pallas_call(
        functools.partial(_ssd_body, nv=nv, has_state=has_state, heads=heads, ssmw=ssmw),
        grid=(b, nchunks),
        in_specs=in_specs,
        out_specs=[blk(ssmw), per_seq(taps - 1, convd), per_seq(ssmw, D_STATE)],
        out_shape=[jax.ShapeDtypeStruct((b, l, ssmw), F32),
                   jax.ShapeDtypeStruct((b, taps - 1, convd), F32),
                   jax.ShapeDtypeStruct((b, ssmw, D_STATE), F32)],
        scratch_shapes=scratch,
        compiler_params=_cparams(("arbitrary", "arbitrary")),
        name="ssd_sample" if has_state else "ssd_prompt",
    )(*args)


def _outproj_body(o_ref, y_ref, x_ref, gsb_ref, wsb_ref, wy_ref, x1_ref):
    o = _rms(o_ref[...], gsb_ref[...]).astype(BF16)
    x1_ref[...] = x_ref[...] + _dot(o, wsb_ref[...]) + _dot(y_ref[...].astype(BF16), wy_ref[...])


def _outproj(o_sb, y, x2d, gsb, wsb, wy, *, tm):
    t, d = x2d.shape
    row = lambda w: pl.BlockSpec((tm, w), lambda i: (i, 0))
    return pl.pallas_call(
        _outproj_body,
        grid=(t // tm,),
        in_specs=[row(o_sb.shape[1]), row(y.shape[1]), row(d), _const_spec(gsb.shape),
                  _const_spec(wsb.shape), _const_spec(wy.shape)],
        out_specs=row(d),
        out_shape=jax.ShapeDtypeStruct((t, d), F32),
        compiler_params=_cparams(("arbitrary",)),
        name="outproj",
    )(o_sb, y, x2d, gsb, wsb, wy)


def _ffn_tail(x1, gate, g1, g2, val, wc_ref, bc_ref, wdown_ref, gfin_ref):
    gc = bc_ref[...] + wc_ref[2:3, :] * gate + wc_ref[1:2, :] * g1 + wc_ref[0:1, :] * g2
    u = (_silu(gc) * val).astype(BF16)
    return _rms(x1 + _dot(u, wdown_ref[...]), gfin_ref[...])


def _ffn_prompt_body(x1_ref, gffn_ref, wup_ref, wc_ref, bc_ref, wdown_ref, gfin_ref,
                     y_ref, fnew_ref, gbuf, *, tm, dff):
    j = pl.program_id(1)

    @pl.when(j == 0)
    def _():
        gbuf[0:SUBLANES, :] = jnp.zeros((SUBLANES, dff), F32)

    x1 = x1_ref[0]
    up = _dot(_rms(x1, gffn_ref[...]).astype(BF16), wup_ref[...])
    gate = up[:, :dff]
    gbuf[SUBLANES:SUBLANES + tm, :] = gate
    g1 = gbuf[SUBLANES - 1:SUBLANES - 1 + tm, :]
    g2 = gbuf[SUBLANES - 2:SUBLANES - 2 + tm, :]
    y_ref[0] = _ffn_tail(x1, gate, g1, g2, up[:, dff:], wc_ref, bc_ref, wdown_ref, gfin_ref)
    gbuf[0:SUBLANES, :] = gbuf[tm:tm + SUBLANES, :]

    @pl.when(j == pl.num_programs(1) - 1)
    def _():
        fnew_ref[0] = gbuf[SUBLANES - 2:SUBLANES, :]


def _ffn_prompt(x1, gffn, wup, wc, bc, wdown, gfin, *, tm):
    b, l, d = x1.shape
    dff = wdown.shape[0]
    blk = pl.BlockSpec((1, tm, d), lambda bi, j: (bi, j, 0))
    const = lambda a: _const_spec(a.shape)
    return pl.pallas_call(
        functools.partial(_ffn_prompt_body, tm=tm, dff=dff),
        grid=(b, l // tm),
        in_specs=[blk] + [const(a) for a in (gffn, wup, wc, bc, wdown, gfin)],
        out_specs=[blk, pl.BlockSpec((1, 2, dff), lambda bi, j: (bi, 0, 0))],
        out_shape=[jax.ShapeDtypeStruct((b, l, d), F32), jax.ShapeDtypeStruct((b, 2, dff), F32)],
        scratch_shapes=[pltpu.VMEM((tm + SUBLANES, dff), F32)],
        compiler_params=_cparams(("arbitrary", "arbitrary")),
        name="ffn_prompt",
    )(x1, gffn, wup, wc, bc, wdown, gfin)


def _ffn_sample_body(x1_ref, p1_ref, p2_ref, gffn_ref, wup_ref, wc_ref, bc_ref, wdown_ref, gfin_ref,
                     y_ref, gate_ref, *, seq, dff):
    x1 = x1_ref[...]
    up = _dot(_rms(x1, gffn_ref[...]).astype(BF16), wup_ref[...])
    gate = up[:, :dff]
    gate_ref[...] = gate
    pos = lax.broadcasted_iota(jnp.int32, (x1.shape[0], 1), 0) % seq
    g1 = jnp.where(pos >= 1, pltpu.roll(gate, 1, axis=0), p1_ref[...])
    g2 = jnp.where(pos >= 2, pltpu.roll(gate, 2, axis=0), p2_ref[...])
    y_ref[...] = _ffn_tail(x1, gate, g1, g2, up[:, dff:], wc_ref, bc_ref, wdown_ref, gfin_ref)


def _ffn_sample(x1, p1, p2, gffn, wup, wc, bc, wdown, gfin, *, seq, tm):
    t, d = x1.shape
    dff = wdown.shape[0]
    row = lambda w: pl.BlockSpec((tm, w), lambda i: (i, 0))
    return pl.pallas_call(
        functools.partial(_ffn_sample_body, seq=seq, dff=dff),
        grid=(t // tm,),
        in_specs=[row(d), row(dff), row(dff)] + [_const_spec(a.shape) for a in (gffn, wup, wc, bc, wdown, gfin)],
        out_specs=[row(d), row(dff)],
        out_shape=[jax.ShapeDtypeStruct((t, d), F32), jax.ShapeDtypeStruct((t, dff), F32)],
        compiler_params=_cparams(("arbitrary",)),
        name="ffn_sample",
    )(x1, p1, p2, gffn, wup, wc, bc, wdown, gfin)


def _pad_lanes(a):
    return jnp.pad(a, ((0, 0), (0, LANES - a.shape[1])))


def kernel(x_prompt, x_sample, cache_k, cache_v, state_ssm, state_conv, state_ffn_conv, page_table,
           g_mix, w_in, sb_bias, g_sb_out, w_conv, b_conv, dt_bias, a_log, d_skip, g_ssm, w_out,
           g_ffn, w_up, w_ffn_conv, b_ffn_conv, w_down, g_final):
    assert g_mix.shape[0] == 1, "single-layer step"
    bp, lp, d = x_prompt.shape
    bd, ld, _ = x_sample.shape
    n_pool, page, _, sb_heads, _ = cache_k.shape
    n_pages = page_table.shape[1]
    assert page == CHUNK
    sbw = sb_heads * HEAD_DIM
    heads = a_log.shape[1]
    ssmw = heads * HEAD_DIM
    convd = w_conv.shape[2]
    dff = w_down.shape[1]

    w = w_in[0]
    wqkv = w[:, :3 * sbw].astype(BF16)
    wz = w[:, 3 * sbw:3 * sbw + ssmw].astype(BF16)
    wxbc = w[:, 3 * sbw + ssmw:3 * sbw + ssmw + convd].astype(BF16)
    wdt = _pad_lanes(w[:, 3 * sbw + ssmw + convd:]).astype(BF16)
    wsb = w_out[0, :sbw].astype(BF16)
    wy = w_out[0, sbw:].astype(BF16)
    wup = w_up[0].astype(BF16)
    wdown = w_down[0].astype(BF16)
    ssd_params = (w_conv[0], b_conv, _pad_lanes(dt_bias), _pad_lanes(a_log),
                  jnp.repeat(d_skip, HEAD_DIM, axis=1), g_ssm)
    ltri = jnp.tril(jnp.ones((CHUNK, CHUNK), BF16))
    w2 = _suffix_matrix()
    bias2 = sb_bias[0] * LOG2E
    npairs = sbw // LANES
    parts = jnp.stack(_split3(bias2)).reshape(3, npairs, 2, 1)
    btile = jnp.zeros((npairs, LANES, 2 * CHUNK), BF16).at[:, :3].set(
        jnp.broadcast_to(parts, (3, npairs, 2, CHUNK)).reshape(3, npairs, 2 * CHUNK).transpose(1, 0, 2))

    wkvt = w[:, sbw:3 * sbw].T.astype(BF16)

    def mix_in(x2d, tm, seq_len=None):
        return _inproj(x2d, g_mix, wqkv, wz, wxbc, wdt, wkvt, sbw=sbw, tm=min(tm, x2d.shape[0]),
                       seq_len=seq_len)

    xp2 = x_prompt.reshape(bp * lp, d)
    q, v_tok, z, xbc, dt, kt_p, vt_p = mix_in(xp2, 256, seq_len=lp)
    to_seq = lambda a: a.reshape(bp, lp, a.shape[-1])
    o_sb = _sb_prompt(to_seq(q), kt_p, to_seq(v_tok), btile, w2)
    y, conv_p, ssm_p = _ssd(to_seq(xbc), to_seq(z), to_seq(dt), ssd_params, ltri, heads=heads)
    x1 = _outproj(o_sb.reshape(bp * lp, sbw), y.reshape(bp * lp, ssmw), xp2, g_sb_out, wsb, wy, tm=512)
    y_prompt, ffn_p = _ffn_prompt(x1.reshape(bp, lp, d), g_ffn, wup, w_ffn_conv[0], b_ffn_conv, wdown,
                                  g_final[None], tm=256)

    ts = bd * ld
    xs2 = x_sample.reshape(ts, d)
    q, v_s, z, xbc, dt, k_s = mix_in(xs2, 256)
    to_seq = lambda a: a.reshape(bd, ld, a.shape[-1])
    bias_rows = jnp.broadcast_to(jnp.tile(bias2, ld)[:, None], (ld * sb_heads, 2 * CHUNK))
    pool_t = lambda c: jnp.transpose(c[:, :, 0], (0, 2, 3, 1)).reshape(n_pool, sbw, page)
    o_rows = _sb_sample(to_seq(q), to_seq(k_s), to_seq(v_s), bias_rows, w2, pool_t(cache_k), pool_t(cache_v),
                        page_table, heads=sb_heads, pages=max(g for g in (2, 4, 8, 16) if n_pages % g == 0))
    y, conv_s, ssm_s = _ssd(to_seq(xbc), to_seq(z), to_seq(dt), ssd_params, ltri, heads=heads,
                            state=(jnp.pad(state_conv[:, 0], ((0, 0), (SUBLANES - state_conv.shape[2], 0), (0, 0))),
                                   state_ssm.reshape(bd, ssmw, D_STATE)))
    x1 = _outproj(o_rows.reshape(ts, sbw), y.reshape(ts, ssmw), xs2, g_sb_out, wsb, wy, tm=ts)
    prev = state_ffn_conv[:, 0]
    p1 = jnp.pad(prev[:, 1:2], ((0, 0), (0, ld - 1), (0, 0))).reshape(ts, dff)
    p2 = jnp.pad(prev, ((0, 0), (0, ld - 2), (0, 0))).reshape(ts, dff)
    y_sample, gate = _ffn_sample(x1, p1, p2, g_ffn, wup, w_ffn_conv[0], b_ffn_conv, wdown, g_final[None],
                                 seq=ld, tm=ts)
    ffn_s = gate.reshape(bd, ld, dff)[:, ld - 2:]

    kv_p = lambda a: jnp.transpose(a.reshape(bp, 1, sb_heads, HEAD_DIM, lp), (0, 4, 1, 2, 3))
    kv_s = lambda a: a.reshape(bd, ld, 1, sb_heads, HEAD_DIM)
    return (y_prompt, y_sample.reshape(bd, ld, d),
            kv_p(kt_p), kv_p(vt_p), ssm_p.reshape(bp, 1, heads, HEAD_DIM, D_STATE),
            conv_p[:, None], ffn_p[:, None],
            kv_s(k_s), kv_s(v_s), ssm_s.reshape(bd, 1, heads, HEAD_DIM, D_STATE),
            conv_s[:, None], ffn_s[:, None])
```

```python
import functools

import jax
import jax.numpy as jnp
from jax import lax
from jax.experimental import pallas as pl
from jax.experimental.pallas import tpu as pltpu

F32 = jnp.float32
BF16 = jnp.bfloat16

HEAD_DIM = 64
D_STATE = 128
SSM_GROUPS = 4
EPS = 1e-6
LOG2E = 1.4426950408889634
LANES = 128
SUBLANES = 8
BF16_ROWS = 16
CHUNK = 128
VMEM_LIMIT = 56 * 1024 * 1024


def _cparams(sem):
    return pltpu.CompilerParams(dimension_semantics=sem, vmem_limit_bytes=VMEM_LIMIT)


def _const_spec(shape):
    nd = len(shape)
    return pl.BlockSpec(shape, lambda *_: (0,) * nd, pipeline_mode=pl.Buffered(1))


def _rms(x, g):
    ms = jnp.mean(x * x, axis=-1, keepdims=True)
    return x * lax.rsqrt(ms + EPS) * g


def _silu(x):
    return x / (1.0 + jnp.exp(-x))


def _dot(a, b):
    return jnp.dot(a, b, preferred_element_type=F32)


def _dot_nt(a, b):
    return lax.dot_general(a, b, (((1,), (1,)), ((), ())), preferred_element_type=F32)


def _split3(x):
    hi = x.astype(BF16)
    r1 = x - hi.astype(F32)
    mid = r1.astype(BF16)
    lo = (r1 - mid.astype(F32)).astype(BF16)
    return hi, mid, lo


def _inproj_body(x_ref, g_ref, wqkv_ref, wz_ref, wxbc_ref, wdt_ref, *rest, sbw, scale, transposed_kv):
    h = _rms(x_ref[...], g_ref[...]).astype(BF16)
    qkv = _dot(h, wqkv_ref[...])
    if transposed_kv:
        wkvt_ref, q_ref, v_ref, z_ref, xbc_ref, dt_ref, kt_ref, vt_ref = rest
        kvt = _dot_nt(wkvt_ref[...], h)
        kt_ref[0] = kvt[:sbw]
        vt_ref[0] = kvt[sbw:]
    else:
        q_ref, v_ref, z_ref, xbc_ref, dt_ref, k_ref = rest
        k_ref[...] = qkv[:, sbw:2 * sbw]
    q_ref[...] = qkv[:, :sbw] * scale
    v_ref[...] = qkv[:, 2 * sbw:]
    for ref, w_ref in ((z_ref, wz_ref), (xbc_ref, wxbc_ref)):
        res = _dot(h, w_ref[...])
        if transposed_kv:
            for j in range(ref.shape[0]):
                ref[j] = res[:, j * LANES:(j + 1) * LANES]
        else:
            ref[...] = res
    dt_ref[...] = _dot(h, wdt_ref[...])


def _inproj(x2d, g, wqkv, wz, wxbc, wdt, wkvt, *, sbw, tm, seq_len=None):
    t, d = x2d.shape
    ssmw, convd = wz.shape[1], wxbc.shape[1]
    row = lambda w: pl.BlockSpec((tm, w), lambda i: (i, 0))
    in_specs = [row(d), _const_spec((1, d)), _const_spec(wqkv.shape), _const_spec(wz.shape),
                _const_spec(wxbc.shape), _const_spec(wdt.shape)]
    args = [x2d, g, wqkv, wz, wxbc, wdt]
    out_specs = [row(sbw), row(sbw), row(ssmw), row(convd), row(LANES)]
    out_shape = [jax.ShapeDtypeStruct((t, w), F32) for w in (sbw, sbw, ssmw, convd, LANES)]
    if seq_len is not None:
        for n, w in ((2, ssmw), (3, convd)):
            out_specs[n] = pl.BlockSpec((w // LANES, tm, LANES), lambda i: (0, i, 0))
            out_shape[n] = jax.ShapeDtypeStruct((w // LANES, t, LANES), F32)
    if seq_len is None:
        out_specs.append(row(sbw))
        out_shape.append(jax.ShapeDtypeStruct((t, sbw), F32))
    else:
        nb = seq_len // tm
        in_specs.append(_const_spec(wkvt.shape))
        args.append(wkvt)
        tr = pl.BlockSpec((1, sbw, tm), lambda i: (i // nb, 0, i % nb))
        out_specs += [tr, tr]
        out_shape += [jax.ShapeDtypeStruct((t // seq_len, sbw, seq_len), F32)] * 2
    return pl.pallas_call(
        functools.partial(_inproj_body, sbw=sbw, scale=HEAD_DIM ** -0.5 * LOG2E, transposed_kv=seq_len is not None),
        grid=(t // tm,),
        in_specs=in_specs,
        out_specs=out_specs,
        out_shape=out_shape,
        compiler_params=_cparams(("arbitrary",)),
        name="inproj",
    )(*args)


def _suffix_matrix():
    r = jnp.arange(2 * CHUNK)[:, None] % CHUNK
    c = jnp.arange(2 * CHUNK)[None, :]
    return jnp.where(c < CHUNK, r >= c, True).astype(BF16)


def _log2_one_minus_sigmoid(z2):
    return jnp.log(1.0 + jnp.exp2(-jnp.abs(z2))) * (-LOG2E) - jnp.maximum(z2, 0.0)


def _split2(x):
    hi = x.astype(BF16)
    return hi, (x - hi.astype(F32)).astype(BF16)


def _sb_prompt_body(q_ref, kt_ref, v_ref, btile_ref, w2_ref, o_ref, kbd_ref, vbd_ref, carry_ref, acc_ref):
    i = pl.program_id(1)
    tq = CHUNK
    npairs, nblk = kbd_ref.shape[0], kbd_ref.shape[1]
    low = lax.broadcasted_iota(jnp.int32, (tq, LANES), 1) < HEAD_DIM

    @pl.when(i == 0)
    def _():
        zero_k = jnp.zeros((HEAD_DIM, tq), BF16)
        for pr in range(npairs):
            rows = slice(pr * LANES, (pr + 1) * LANES)
            for j in range(nblk):
                keys = slice(j * tq, (j + 1) * tq)
                k2 = kt_ref[0, rows, keys].astype(BF16)
                kbd_ref[pr, j, 0:HEAD_DIM, :] = jnp.concatenate([k2[:HEAD_DIM], zero_k], axis=1)
                kbd_ref[pr, j, HEAD_DIM:LANES, :] = jnp.concatenate([zero_k, k2[HEAD_DIM:]], axis=1)
                kbd_ref[pr, j, LANES:, :] = btile_ref[pr]
                v2 = v_ref[0, keys, rows]
                vbd_ref[pr, j, 0:tq, :] = jnp.where(low, v2, 0.0).astype(BF16)
                vbd_ref[pr, j, tq:, :] = jnp.where(low, 0.0, v2).astype(BF16)

    lane = lax.broadcasted_iota(jnp.int32, (tq, LANES), 1)
    bias_cols = jnp.where(lane < 3, 1.0, 0.0)
    qx = [jnp.concatenate([q_ref[0, :, pr * LANES:(pr + 1) * LANES], bias_cols], axis=1).astype(BF16)
          for pr in range(npairs)]
    causal = (lax.broadcasted_iota(jnp.int32, (tq, tq), 1) < lax.broadcasted_iota(jnp.int32, (tq, tq), 0))
    causal2 = jnp.concatenate([causal, causal], axis=1)
    w2 = w2_ref[...]

    def tile(jbs, first):
        zs, parts = [], []
        for pr in range(npairs):
            for jb in jbs:
                zz = _dot(qx[pr], kbd_ref[pr, jb])
                l = _log2_one_minus_sigmoid(zz)
                if first:
                    l = jnp.where(causal2, l, 0.0)
                hi, lo = _split2(l)
                for h in range(2):
                    cols = slice(h * tq, (h + 1) * tq)
                    zs.append(zz[:, cols])
                    parts.append(jnp.concatenate([hi[:, cols], lo[:, cols]], axis=1))
        r = _dot(jnp.concatenate(parts, axis=0), w2)
        per_pair = 2 * len(jbs)
        for pr in range(npairs):
            ws = []
            cs = [None, None] if first else [carry_ref[2 * pr], carry_ref[2 * pr + 1]]
            for n in range(pr * per_pair, (pr + 1) * per_pair):
                h = n % 2
                rr = r[n * tq:(n + 1) * tq]
                suf, tot = rr[:, :tq], rr[:, tq:]
                if first:
                    ws.append(jnp.where(causal, jnp.exp2(zs[n] + suf), 0.0))
                    cs[h] = tot
                else:
                    ws.append(jnp.exp2(zs[n] + suf + cs[h]))
                    cs[h] = cs[h] + tot
            carry_ref[2 * pr] = cs[0]
            carry_ref[2 * pr + 1] = cs[1]
            lhs = jnp.concatenate(ws, axis=1).astype(BF16)
            rhs = jnp.concatenate([vbd_ref[pr, jb] for jb in jbs], axis=0)
            if first:
                acc_ref[pr] = _dot(lhs, rhs)
            else:
                acc_ref[pr] += _dot(lhs, rhs)

    tile([i], True)
    odd = i % 2

    @pl.when(odd == 1)
    def _():
        tile([i - 1], False)

    def body(j, carry):
        top = i - odd - 1 - 2 * j
        tile([top, top - 1], False)
        return carry

    lax.fori_loop(0, i // 2, body, 0)
    o_ref[0] = jnp.concatenate([acc_ref[pr] for pr in range(npairs)], axis=1)


def _sb_prompt(q, kt, v, btile, w2):
    b, l, sbw = q.shape
    tq = CHUNK
    npairs = sbw // LANES
    qspec = pl.BlockSpec((1, tq, sbw), lambda bi, i: (bi, i, 0))
    return pl.pallas_call(
        _sb_prompt_body,
        grid=(b, l // tq),
        in_specs=[qspec, pl.BlockSpec((1, sbw, l), lambda bi, i: (bi, 0, 0)),
                  pl.BlockSpec((1, l, sbw), lambda bi, i: (bi, 0, 0)),
                  _const_spec(btile.shape), _const_spec(w2.shape)],
        out_specs=qspec,
        out_shape=jax.ShapeDtypeStruct((b, l, sbw), F32),
        scratch_shapes=[pltpu.VMEM((npairs, l // tq, 2 * LANES, 2 * tq), BF16),
                        pltpu.VMEM((npairs, l // tq, 2 * tq, LANES), BF16),
                        pltpu.VMEM((2 * npairs, tq, LANES), F32),
                        pltpu.VMEM((npairs, tq, LANES), F32)],
        compiler_params=_cparams(("arbitrary", "arbitrary")),
        name="sb_prompt",
    )(q, kt, v, btile, w2)


def _sb_sample_body(pt_ref, q_ref, kn_ref, vn_ref, bias_ref, w2_ref, *rest, pages, heads, steps):
    kp = rest[:pages]
    vp = rest[pages:2 * pages]
    o_ref, carry_ref, acc_ref = rest[2 * pages:]
    p = pl.program_id(1)
    nt, width = q_ref.shape[1], q_ref.shape[2]
    rows = nt * heads
    q = q_ref[0]
    qrep = jnp.concatenate([jnp.broadcast_to(q[t:t + 1], (heads, width)) for t in range(nt)], axis=0)
    own = (lax.broadcasted_iota(jnp.int32, (rows, width), 1) // HEAD_DIM
           == lax.broadcasted_iota(jnp.int32, (rows, width), 0) % heads)
    qbd = jnp.where(own, qrep, 0.0)
    bias = bias_ref[...]

    @pl.when(p == 0)
    def _():
        kn = kn_ref[0]
        vn = vn_ref[0]
        t_row = lax.broadcasted_iota(jnp.int32, (rows, 1), 0) // heads
        zs, ls = [], []
        for s in range(nt):
            z2 = jnp.sum(qbd * kn[s:s + 1], axis=1, keepdims=True) + bias[:, :1]
            zs.append(z2)
            ls.append(jnp.where(s < t_row, _log2_one_minus_sigmoid(z2), 0.0))
        o = jnp.zeros((rows, width), F32)
        suffix = jnp.zeros((rows, 1), F32)
        for s in reversed(range(nt)):
            w = jnp.where(s < t_row, jnp.exp2(zs[s] + ls[s] + suffix), 0.0)
            o = o + w * vn[s:s + 1]
            suffix = suffix + ls[s]
        carry_ref[...] = jnp.broadcast_to(suffix, (rows, CHUNK))
        acc_ref[...] = o

    w2 = w2_ref[...]
    qb = qbd.astype(BF16)
    carry = carry_ref[...]
    acc = acc_ref[...]
    zs, parts = [], []
    for g in range(0, pages, 2):
        kk = jnp.concatenate([kp[g][0], kp[g + 1][0]], axis=1).astype(BF16)
        zz = _dot(qb, kk) + bias
        hi, lo = _split2(_log2_one_minus_sigmoid(zz))
        for h in range(2):
            cols = slice(h * CHUNK, (h + 1) * CHUNK)
            zs.append(zz[:, cols])
            parts.append(jnp.concatenate([hi[:, cols], lo[:, cols]], axis=1))
    r = _dot(jnp.concatenate(parts, axis=0), w2)
    ws = []
    for g in range(pages):
        rr = r[g * rows:(g + 1) * rows]
        ws.append(jnp.exp2(zs[g] + rr[:, :CHUNK] + carry))
        carry = carry + rr[:, CHUNK:]
    for g in range(0, pages, 2):
        vv = jnp.concatenate([vp[g][0], vp[g + 1][0]], axis=1).astype(BF16)
        acc = acc + _dot_nt(jnp.concatenate(ws[g:g + 2], axis=1).astype(BF16), vv)
    carry_ref[...] = carry
    acc_ref[...] = acc

    @pl.when(p == steps - 1)
    def _():
        mine = jnp.where(own, acc, 0.0)
        o_ref[0] = jnp.concatenate(
            [jnp.sum(mine[t * heads:(t + 1) * heads], axis=0, keepdims=True) for t in range(nt)], axis=0)


def _sb_sample(q, kn, vn, bias_rows, w2, cache_kt, cache_vt, page_table, *, heads, pages):
    bd, nt, width = q.shape
    rows = nt * heads
    n_pages = page_table.shape[1]
    steps = n_pages // pages
    page = cache_kt.shape[2]

    def page_spec(g):
        def imap(b, p, pt):
            return (pt[b, n_pages - 1 - (p * pages + g)], 0, 0)
        return pl.BlockSpec((1, width, page), imap)

    seq = pl.BlockSpec((1, nt, width), lambda b, p, pt: (b, 0, 0))
    grid_spec = pltpu.PrefetchScalarGridSpec(
        num_scalar_prefetch=1,
        grid=(bd, steps),
        in_specs=[seq, seq, seq, _const_spec(bias_rows.shape), _const_spec(w2.shape)]
        + [page_spec(g) for g in range(pages)] * 2,
        out_specs=seq,
        scratch_shapes=[pltpu.VMEM((rows, CHUNK), F32), pltpu.VMEM((rows, width), F32)],
    )
    return pl.pallas_call(
        functools.partial(_sb_sample_body, pages=pages, heads=heads, steps=steps),
        grid_spec=grid_spec,
        out_shape=jax.ShapeDtypeStruct((bd, nt, width), F32),
        compiler_params=_cparams(("arbitrary", "arbitrary")),
        name="sb_sample",
    )(page_table, q, kn, vn, bias_rows, w2, *([cache_kt] * pages), *([cache_vt] * pages))


def _ssd_body(*refs, cl, nv, has_state, heads, ssmw):
    if has_state:
        (xbc_ref, z_ref, dt_ref, cprev_ref, h0_ref, wconv_ref, bconv_ref, dtb_ref, alog_ref,
         dskip_ref, gssm_ref, y_ref, conv_ref, ssm_ref, ht_ref, xstage, zstage, dtstage) = refs
    else:
        (xbc_ref, z_ref, dt_ref, wconv_ref, bconv_ref, dtb_ref, alog_ref,
         dskip_ref, gssm_ref, y_ref, conv_ref, ssm_ref, ht_ref, xprev) = refs
    c = pl.program_id(1)
    last = pl.num_programs(1) - 1
    taps, convd = wconv_ref.shape
    ncol_x, ncol_s = convd // LANES, ssmw // LANES
    hpg = heads // SSM_GROUPS
    ntile = cl // SUBLANES
    assert nv >= taps - 1 and D_STATE == LANES and 2 * HEAD_DIM == LANES
    time_of = lambda p: ntile * (p % SUBLANES) + p // SUBLANES
    row_of = lambda t: SUBLANES * (t % ntile) + t // ntile
    sub = lax.broadcasted_iota(jnp.int32, (SUBLANES, LANES), 0)
    reach = sorted({(a - k) % ntile for a in range(ntile) for k in range(1, taps) if a - k < 0})

    def staged(ref, stage):
        stage[...] = jnp.zeros_like(stage)
        for j in range(stage.shape[0]):
            stage[j, 0:nv, :] = ref[0, :, j * LANES:(j + 1) * LANES]
        return stage

    def tiles(src, j):
        return [src[j, pl.ds(a, SUBLANES, stride=ntile), :] for a in range(ntile)]

    if has_state:
        xsrc, zsrc, dsrc = staged(xbc_ref, xstage), staged(z_ref, zstage), staged(dt_ref, dtstage)
        ht_ref[...] = h0_ref[0].T
    else:
        xsrc, zsrc, dsrc = xbc_ref, z_ref, dt_ref

        @pl.when(c == 0)
        def _():
            xprev[...] = jnp.zeros_like(xprev)
            ht_ref[...] = jnp.zeros_like(ht_ref)

    xcols, tail_rows = [], []
    for j in range(ncol_x):
        lanes = slice(j * LANES, (j + 1) * LANES)
        cur = tiles(xsrc, j)
        if has_state:
            prev = {}
            for a2 in reach:
                tile = jnp.zeros((SUBLANES, LANES), F32)
                for s in range(SUBLANES):
                    t = ntile * (s - SUBLANES) + a2
                    if t >= -(taps - 1):
                        row = cprev_ref[0, SUBLANES + t:SUBLANES + t + 1, lanes]
                        tile = jnp.where(sub == s, jnp.broadcast_to(row, (SUBLANES, LANES)), tile)
                prev[a2] = tile
        else:
            prev = {a2: xprev[n, :, lanes] for n, a2 in enumerate(reach)}
        w = [wconv_ref[taps - 1 - k:taps - k, lanes] for k in range(taps)]
        pieces = []
        for a in range(ntile):
            acc = bconv_ref[:, lanes] + cur[a] * w[0]
            for k in range(1, taps):
                a2, r = (a - k) % ntile, -((a - k) // ntile)
                if r == 0:
                    sh = cur[a2]
                else:
                    sh = jnp.where(sub < r, pltpu.roll(prev[a2], r, axis=0), pltpu.roll(cur[a2], r, axis=0))
                acc = acc + sh * w[k]
            pieces.append(acc)
        xcols.append(_silu(jnp.concatenate(pieces, axis=0)))
        tail_rows.append(jnp.concatenate(
            [cur[t % ntile][t // ntile:t // ntile + 1] for t in range(nv - (taps - 1), nv)], axis=0))
        if not has_state:
            for n, a2 in enumerate(reach):
                xprev[n, :, lanes] = cur[a2]
    xs_cols = xcols[:ncol_s]
    bm_cols = xcols[ncol_s:ncol_s + SSM_GROUPS]
    cm_cols = xcols[ncol_s + SSM_GROUPS:]

    @pl.when(c == last)
    def _():
        conv_ref[0] = jnp.concatenate(tail_rows, axis=1)

    x = jnp.concatenate(tiles(dsrc, 0), axis=0) + dtb_ref[...]
    dt = jnp.maximum(x, 0.0) + jnp.log1p(jnp.exp(-jnp.abs(x)))
    if nv < cl:
        dt = jnp.where(time_of(lax.broadcasted_iota(jnp.int32, (cl, LANES), 0)) < nv, dt, 0.0)
    a = -jnp.exp(alog_ref[...])
    da = dt * a
    causal = (time_of(lax.broadcasted_iota(jnp.int32, (cl, cl), 1))
              <= time_of(lax.broadcasted_iota(jnp.int32, (cl, cl), 0)))
    ltri = jnp.where(causal, 1.0, 0.0).astype(BF16)
    acum = sum(_dot(ltri, part) for part in _split3(da))
    acum_last = acum[cl - 1:cl, :]
    eac = jnp.exp(acum)
    wgt = dt * jnp.exp(acum_last - acum)
    cdec = jnp.exp(acum_last)
    acum_t = acum.T
    dt_t = dt.T
    wgt_t = wgt.T

    low = lax.broadcasted_iota(jnp.int32, (cl, LANES), 1) < HEAD_DIM
    low1 = low[:1]
    low_n = lax.broadcasted_iota(jnp.int32, (D_STATE, LANES), 1) < HEAD_DIM

    ycols = []
    for g in range(SSM_GROUPS):
        cc = cm_cols[g]
        bc = bm_cols[g]
        cb = _dot_nt(cc.astype(BF16), bc.astype(BF16))
        bc_t = bc.T
        for jp in range(hpg // 2):
            h0 = g * hpg + 2 * jp
            col = h0 // 2
            lanes = slice(col * LANES, (col + 1) * LANES)
            xs_p = xs_cols[col]
            xs_pb = xs_p.astype(BF16)
            ht_p = ht_ref[:, lanes]
            ht_pb = ht_p.astype(BF16)
            yh, sh = [], []
            for hh in (h0, h0 + 1):
                seg = acum[:, hh:hh + 1] - acum_t[hh:hh + 1, :]
                m = (cb * jnp.where(causal, jnp.exp(seg), 0.0) * dt_t[hh:hh + 1, :]).astype(BF16)
                off = (cc * eac[:, hh:hh + 1]).astype(BF16)
                if cl == LANES:
                    yh.append(_dot(jnp.concatenate([m, off], axis=1), jnp.concatenate([xs_pb, ht_pb], axis=0)))
                else:
                    yh.append(_dot(m, xs_pb) + _dot(off, ht_pb))
                sh.append(_dot((bc_t * wgt_t[hh:hh + 1, :]).astype(BF16), xs_pb))
            cd = jnp.where(low1, cdec[:, h0:h0 + 1], cdec[:, h0 + 1:h0 + 2])
            ht_ref[:, lanes] = cd * ht_p + jnp.where(low_n, sh[0], sh[1])
            zin = jnp.concatenate(tiles(zsrc, col), axis=0)
            ycols.append((jnp.where(low, yh[0], yh[1]) + dskip_ref[:, lanes] * xs_p) * _silu(zin))

    per_group = ncol_s // SSM_GROUPS
    for g in range(SSM_GROUPS):
        group = ycols[g * per_group:(g + 1) * per_group]
        ms = sum(jnp.sum(yc * yc, axis=-1, keepdims=True) for yc in group) * (1.0 / (per_group * LANES))
        scale = lax.rsqrt(ms + EPS)
        for n, yc in enumerate(group):
            col = g * per_group + n
            lanes = slice(col * LANES, (col + 1) * LANES)
            out = yc * scale * gssm_ref[:, lanes]
            if nv < cl:
                y_ref[0, :, lanes] = jnp.concatenate([out[row_of(t):row_of(t) + 1] for t in range(nv)], axis=0)
            else:
                for a in range(ntile):
                    y_ref[col, pl.ds(a, SUBLANES, stride=ntile), :] = out[a * SUBLANES:(a + 1) * SUBLANES]

    @pl.when(c == last)
    def _():
        ssm_ref[0] = ht_ref[...].T


def _ssd(xbc, z, dt, params, *, heads, cl, batch, state=None):
    has_state = state is not None
    l = dt.shape[1]
    taps, convd = params[0].shape
    ssmw = heads * HEAD_DIM
    per_seq = lambda r, w: pl.BlockSpec((1, r, w), lambda bi, c: (bi, 0, 0))
    const = lambda a: _const_spec(a.shape)
    scratch = [pltpu.VMEM((D_STATE, ssmw), F32)]
    if has_state:
        nv, nchunks = l, 1
        blk = lambda w: pl.BlockSpec((1, nv, w), lambda bi, c: (bi, 0, 0))
        y_shape = (batch, l, ssmw)
        in_specs = [blk(convd), blk(ssmw), blk(LANES), per_seq(SUBLANES, convd), per_seq(ssmw, D_STATE)]
        args = [xbc, z, dt] + list(state)
        scratch += [pltpu.VMEM((w // LANES, cl, LANES), F32) for w in (convd, ssmw, LANES)]
    else:
        nv, nchunks = cl, l // cl
        blk = lambda w: pl.BlockSpec((w // LANES, cl, LANES), lambda bi, c: (0, bi * nchunks + c, 0))
        y_shape = (ssmw // LANES, batch * l, LANES)
        in_specs = [blk(convd), blk(ssmw), pl.BlockSpec((1, cl, LANES), lambda bi, c: (bi, c, 0))]
        args = [xbc, z, dt]
        scratch += [pltpu.VMEM((min(taps - 1, cl // SUBLANES), SUBLANES, convd), F32)]
    in_specs += [const(p) for p in params]
    args += list(params)
    return pl.pallas_call(
        functools.partial(_ssd_body, cl=cl, nv=nv, has_state=has_state, heads=heads, ssmw=ssmw),
        grid=(batch, nchunks),
        in_specs=in_specs,
        out_specs=[blk(ssmw), per_seq(taps - 1, convd), per_seq(ssmw, D_STATE)],
        out_shape=[jax.ShapeDtypeStruct(y_shape, F32),
                   jax.ShapeDtypeStruct((batch, taps - 1, convd), F32),
                   jax.ShapeDtypeStruct((batch, ssmw, D_STATE), F32)],
        scratch_shapes=scratch,
        compiler_params=_cparams(("arbitrary", "arbitrary")),
        name="ssd_sample" if has_state else "ssd_prompt",
    )(*args)


def _outproj_body(o_ref, y_ref, x_ref, gsb_ref, wsb_ref, wy_ref, x1_ref):
    o = _rms(o_ref[...], gsb_ref[...]).astype(BF16)
    if len(y_ref.shape) == 3:
        y = jnp.concatenate([y_ref[j] for j in range(y_ref.shape[0])], axis=1)
    else:
        y = y_ref[...]
    x1_ref[...] = x_ref[...] + _dot(o, wsb_ref[...]) + _dot(y.astype(BF16), wy_ref[...])


def _outproj(o_sb, y, x2d, gsb, wsb, wy, *, tm):
    t, d = x2d.shape
    row = lambda w: pl.BlockSpec((tm, w), lambda i: (i, 0))
    y_spec = row(y.shape[1]) if y.ndim == 2 else pl.BlockSpec((y.shape[0], tm, LANES), lambda i: (0, i, 0))
    return pl.pallas_call(
        _outproj_body,
        grid=(t // tm,),
        in_specs=[row(o_sb.shape[1]), y_spec, row(d), _const_spec(gsb.shape),
                  _const_spec(wsb.shape), _const_spec(wy.shape)],
        out_specs=row(d),
        out_shape=jax.ShapeDtypeStruct((t, d), F32),
        compiler_params=_cparams(("arbitrary",)),
        name="outproj",
    )(o_sb, y, x2d, gsb, wsb, wy)


def _ffn_tail(x1, gate, g1, g2, val, wc_ref, bc_ref, wdown_ref, gfin_ref):
    gc = bc_ref[...] + wc_ref[2:3, :] * gate + wc_ref[1:2, :] * g1 + wc_ref[0:1, :] * g2
    u = (_silu(gc) * val).astype(BF16)
    return _rms(x1 + _dot(u, wdown_ref[...]), gfin_ref[...])


def _ffn_prompt_body(x1_ref, gffn_ref, wup_ref, wc_ref, bc_ref, wdown_ref, gfin_ref,
                     y_ref, fnew_ref, gbuf, *, tm, dff):
    j = pl.program_id(1)

    @pl.when(j == 0)
    def _():
        gbuf[0:SUBLANES, :] = jnp.zeros((SUBLANES, dff), F32)

    x1 = x1_ref[0]
    up = _dot(_rms(x1, gffn_ref[...]).astype(BF16), wup_ref[...])
    gate = up[:, :dff]
    gbuf[SUBLANES:SUBLANES + tm, :] = gate
    g1 = gbuf[SUBLANES - 1:SUBLANES - 1 + tm, :]
    g2 = gbuf[SUBLANES - 2:SUBLANES - 2 + tm, :]
    y_ref[0] = _ffn_tail(x1, gate, g1, g2, up[:, dff:], wc_ref, bc_ref, wdown_ref, gfin_ref)
    gbuf[0:SUBLANES, :] = gbuf[tm:tm + SUBLANES, :]

    @pl.when(j == pl.num_programs(1) - 1)
    def _():
        fnew_ref[0] = gbuf[SUBLANES - 2:SUBLANES, :]


def _ffn_prompt(x1, gffn, wup, wc, bc, wdown, gfin, *, tm):
    b, l, d = x1.shape
    dff = wdown.shape[0]
    blk = pl.BlockSpec((1, tm, d), lambda bi, j: (bi, j, 0))
    const = lambda a: _const_spec(a.shape)
    return pl.pallas_call(
        functools.partial(_ffn_prompt_body, tm=tm, dff=dff),
        grid=(b, l // tm),
        in_specs=[blk] + [const(a) for a in (gffn, wup, wc, bc, wdown, gfin)],
        out_specs=[blk, pl.BlockSpec((1, 2, dff), lambda bi, j: (bi, 0, 0))],
        out_shape=[jax.ShapeDtypeStruct((b, l, d), F32), jax.ShapeDtypeStruct((b, 2, dff), F32)],
        scratch_shapes=[pltpu.VMEM((tm + SUBLANES, dff), F32)],
        compiler_params=_cparams(("arbitrary", "arbitrary")),
        name="ffn_prompt",
    )(x1, gffn, wup, wc, bc, wdown, gfin)


def _ffn_sample_body(x1_ref, p1_ref, p2_ref, gffn_ref, wup_ref, wc_ref, bc_ref, wdown_ref, gfin_ref,
                     y_ref, gate_ref, *, seq, dff):
    x1 = x1_ref[...]
    up = _dot(_rms(x1, gffn_ref[...]).astype(BF16), wup_ref[...])
    gate = up[:, :dff]
    gate_ref[...] = gate
    pos = lax.broadcasted_iota(jnp.int32, (x1.shape[0], 1), 0) % seq
    g1 = jnp.where(pos >= 1, pltpu.roll(gate, 1, axis=0), p1_ref[...])
    g2 = jnp.where(pos >= 2, pltpu.roll(gate, 2, axis=0), p2_ref[...])
    y_ref[...] = _ffn_tail(x1, gate, g1, g2, up[:, dff:], wc_ref, bc_ref, wdown_ref, gfin_ref)


def _ffn_sample(x1, p1, p2, gffn, wup, wc, bc, wdown, gfin, *, seq, tm):
    t, d = x1.shape
    dff = wdown.shape[0]
    row = lambda w: pl.BlockSpec((tm, w), lambda i: (i, 0))
    return pl.pallas_call(
        functools.partial(_ffn_sample_body, seq=seq, dff=dff),
        grid=(t // tm,),
        in_specs=[row(d), row(dff), row(dff)] + [_const_spec(a.shape) for a in (gffn, wup, wc, bc, wdown, gfin)],
        out_specs=[row(d), row(dff)],
        out_shape=[jax.ShapeDtypeStruct((t, d), F32), jax.ShapeDtypeStruct((t, dff), F32)],
        compiler_params=_cparams(("arbitrary",)),
        name="ffn_sample",
    )(x1, p1, p2, gffn, wup, wc, bc, wdown, gfin)


def _pad_lanes(a):
    return jnp.pad(a, ((0, 0), (0, LANES - a.shape[1])))


def kernel(x_prompt, x_sample, cache_k, cache_v, state_ssm, state_conv, state_ffn_conv, page_table,
           g_mix, w_in, sb_bias, g_sb_out, w_conv, b_conv, dt_bias, a_log, d_skip, g_ssm, w_out,
           g_ffn, w_up, w_ffn_conv, b_ffn_conv, w_down, g_final):
    assert g_mix.shape[0] == 1, "single-layer step"
    bp, lp, d = x_prompt.shape
    bd, ld, _ = x_sample.shape
    n_pool, page, _, sb_heads, _ = cache_k.shape
    n_pages = page_table.shape[1]
    assert page == CHUNK
    sbw = sb_heads * HEAD_DIM
    heads = a_log.shape[1]
    ssmw = heads * HEAD_DIM
    convd = w_conv.shape[2]
    dff = w_down.shape[1]

    w = w_in[0]
    wqkv = w[:, :3 * sbw].astype(BF16)
    wz = w[:, 3 * sbw:3 * sbw + ssmw].astype(BF16)
    wxbc = w[:, 3 * sbw + ssmw:3 * sbw + ssmw + convd].astype(BF16)
    wdt = _pad_lanes(w[:, 3 * sbw + ssmw + convd:]).astype(BF16)
    wsb = w_out[0, :sbw].astype(BF16)
    wy = w_out[0, sbw:].astype(BF16)
    wup = w_up[0].astype(BF16)
    wdown = w_down[0].astype(BF16)
    ssd_params = (w_conv[0], b_conv, _pad_lanes(dt_bias), _pad_lanes(a_log),
                  jnp.repeat(d_skip, HEAD_DIM, axis=1), g_ssm)
    w2 = _suffix_matrix()
    bias2 = sb_bias[0] * LOG2E
    npairs = sbw // LANES
    parts = jnp.stack(_split3(bias2)).reshape(3, npairs, 2, 1)
    btile = jnp.zeros((npairs, LANES, 2 * CHUNK), BF16).at[:, :3].set(
        jnp.broadcast_to(parts, (3, npairs, 2, CHUNK)).reshape(3, npairs, 2 * CHUNK).transpose(1, 0, 2))

    wkvt = w[:, sbw:3 * sbw].T.astype(BF16)

    def mix_in(x2d, tm, seq_len=None):
        return _inproj(x2d, g_mix, wqkv, wz, wxbc, wdt, wkvt, sbw=sbw, tm=min(tm, x2d.shape[0]),
                       seq_len=seq_len)

    xp2 = x_prompt.reshape(bp * lp, d)
    q, v_tok, z, xbc, dt, kt_p, vt_p = mix_in(xp2, 256, seq_len=lp)
    to_seq = lambda a: a.reshape(bp, lp, a.shape[-1])
    o_sb = _sb_prompt(to_seq(q), kt_p, to_seq(v_tok), btile, w2)
    y, conv_p, ssm_p = _ssd(xbc, z, to_seq(dt), ssd_params, heads=heads, cl=CHUNK, batch=bp)
    x1 = _outproj(o_sb.reshape(bp * lp, sbw), y, xp2, g_sb_out, wsb, wy, tm=512)
    y_prompt, ffn_p = _ffn_prompt(x1.reshape(bp, lp, d), g_ffn, wup, w_ffn_conv[0], b_ffn_conv, wdown,
                                  g_final[None], tm=256)

    ts = bd * ld
    xs2 = x_sample.reshape(ts, d)
    q, v_s, z, xbc, dt, k_s = mix_in(xs2, 256)
    to_seq = lambda a: a.reshape(bd, ld, a.shape[-1])
    bias_rows = jnp.broadcast_to(jnp.tile(bias2, ld)[:, None], (ld * sb_heads, 2 * CHUNK))
    pool_t = lambda c: jnp.transpose(c[:, :, 0], (0, 2, 3, 1)).reshape(n_pool, sbw, page)
    o_rows = _sb_sample(to_seq(q), to_seq(k_s), to_seq(v_s), bias_rows, w2, pool_t(cache_k), pool_t(cache_v),
                        page_table, heads=sb_heads, pages=max(g for g in (2, 4, 8, 16) if n_pages % g == 0))
    cl_s = -(-ld // BF16_ROWS) * BF16_ROWS
    y, conv_s, ssm_s = _ssd(to_seq(xbc), to_seq(z), to_seq(dt), ssd_params, heads=heads, cl=cl_s, batch=bd,
                            state=(jnp.pad(state_conv[:, 0], ((0, 0), (SUBLANES - state_conv.shape[2], 0), (0, 0))),
                                   state_ssm.reshape(bd, ssmw, D_STATE)))
    x1 = _outproj(o_rows.reshape(ts, sbw), y.reshape(ts, ssmw), xs2, g_sb_out, wsb, wy, tm=ts)
    prev = state_ffn_conv[:, 0]
    p1 = jnp.pad(prev[:, 1:2], ((0, 0), (0, ld - 1), (0, 0))).reshape(ts, dff)
    p2 = jnp.pad(prev, ((0, 0), (0, ld - 2), (0, 0))).reshape(ts, dff)
    y_sample, gate = _ffn_sample(x1, p1, p2, g_ffn, wup, w_ffn_conv[0], b_ffn_conv, wdown, g_final[None],
                                 seq=ld, tm=ts)
    ffn_s = gate.reshape(bd, ld, dff)[:, ld - 2:]

    kv_p = lambda a: jnp.transpose(a.reshape(bp, 1, sb_heads, HEAD_DIM, lp), (0, 4, 1, 2, 3))
    kv_s = lambda a: a.reshape(bd, ld, 1, sb_heads, HEAD_DIM)
    return (y_prompt, y_sample.reshape(bd, ld, d),
            kv_p(kt_p), kv_p(vt_p), ssm_p.reshape(bp, 1, heads, HEAD_DIM, D_STATE),
            conv_p[:, None], ffn_p[:, None],
            kv_s(k_s), kv_s(v_s), ssm_s.reshape(bd, 1, heads, HEAD_DIM, D_STATE),
            conv_s[:, None], ffn_s[:, None])
```

```python
import functools

import jax
import jax.numpy as jnp
from jax import lax
from jax.experimental import pallas as pl
from jax.experimental.pallas import tpu as pltpu

F32 = jnp.float32
BF16 = jnp.bfloat16

HEAD_DIM = 64
D_STATE = 128
SSM_GROUPS = 4
EPS = 1e-6
LOG2E = 1.4426950408889634
LANES = 128
SUBLANES = 8
BF16_ROWS = 16
CHUNK = 128
VMEM_LIMIT = 56 * 1024 * 1024

def _cparams(sem):
    return pltpu.CompilerParams(dimension_semantics=sem, vmem_limit_bytes=VMEM_LIMIT)


def _const_spec(shape):
    nd = len(shape)
    return pl.BlockSpec(shape, lambda *_: (0,) * nd, pipeline_mode=pl.Buffered(1))


def _rms(x, g):
    ms = jnp.mean(x * x, axis=-1, keepdims=True)
    return x * lax.rsqrt(ms + EPS) * g


def _silu(x):
    return x / (1.0 + jnp.exp(-x))


def _dot(a, b):
    return jnp.dot(a, b, preferred_element_type=F32)


def _dot_nt(a, b):
    return lax.dot_general(a, b, (((1,), (1,)), ((), ())), preferred_element_type=F32)


def _split3(x):
    hi = x.astype(BF16)
    r1 = x - hi.astype(F32)
    mid = r1.astype(BF16)
    lo = (r1 - mid.astype(F32)).astype(BF16)
    return hi, mid, lo


def _inproj_body(x_ref, g_ref, wqkv_ref, wz_ref, wxbc_ref, wdt_ref, *rest, sbw, scale, transposed_kv):
    h = _rms(x_ref[...], g_ref[...]).astype(BF16)
    qkv = _dot(h, wqkv_ref[...])
    if transposed_kv:
        wkvt_ref, q_ref, v_ref, z_ref, xbc_ref, dt_ref, kt_ref, vt_ref = rest
        kvt = _dot_nt(wkvt_ref[...], h)
        kt_ref[0] = kvt[:sbw]
        vt_ref[0] = kvt[sbw:]
    else:
        q_ref, v_ref, z_ref, xbc_ref, dt_ref, k_ref = rest
        k_ref[...] = qkv[:, sbw:2 * sbw]
    q_ref[...] = qkv[:, :sbw] * scale
    v_ref[...] = qkv[:, 2 * sbw:]
    for ref, w_ref in ((z_ref, wz_ref), (xbc_ref, wxbc_ref)):
        res = _dot(h, w_ref[...])
        if transposed_kv:
            for j in range(ref.shape[0]):
                ref[j] = res[:, j * LANES:(j + 1) * LANES]
        else:
            ref[...] = res
    dt_ref[...] = _dot(h, wdt_ref[...])


def _inproj(x2d, g, wqkv, wz, wxbc, wdt, wkvt, *, sbw, tm, seq_len=None):
    t, d = x2d.shape
    ssmw, convd = wz.shape[1], wxbc.shape[1]
    row = lambda w: pl.BlockSpec((tm, w), lambda i: (i, 0))
    in_specs = [row(d), _const_spec((1, d)), _const_spec(wqkv.shape), _const_spec(wz.shape),
                _const_spec(wxbc.shape), _const_spec(wdt.shape)]
    args = [x2d, g, wqkv, wz, wxbc, wdt]
    out_specs = [row(sbw), row(sbw), row(ssmw), row(convd), row(LANES)]
    out_shape = [jax.ShapeDtypeStruct((t, w), F32) for w in (sbw, sbw, ssmw, convd, LANES)]
    if seq_len is not None:
        for n, w in ((2, ssmw), (3, convd)):
            out_specs[n] = pl.BlockSpec((w // LANES, tm, LANES), lambda i: (0, i, 0))
            out_shape[n] = jax.ShapeDtypeStruct((w // LANES, t, LANES), F32)
    if seq_len is None:
        out_specs.append(row(sbw))
        out_shape.append(jax.ShapeDtypeStruct((t, sbw), F32))
    else:
        nb = seq_len // tm
        in_specs.append(_const_spec(wkvt.shape))
        args.append(wkvt)
        tr = pl.BlockSpec((1, sbw, tm), lambda i: (i // nb, 0, i % nb))
        out_specs += [tr, tr]
        out_shape += [jax.ShapeDtypeStruct((t // seq_len, sbw, seq_len), F32)] * 2
    return pl.pallas_call(
        functools.partial(_inproj_body, sbw=sbw, scale=HEAD_DIM ** -0.5 * LOG2E, transposed_kv=seq_len is not None),
        grid=(t // tm,),
        in_specs=in_specs,
        out_specs=out_specs,
        out_shape=out_shape,
        compiler_params=_cparams(("arbitrary",)),
        name="inproj",
    )(*args)


def _suffix_matrix():
    r = jnp.arange(2 * CHUNK)[:, None] % CHUNK
    c = jnp.arange(2 * CHUNK)[None, :]
    return jnp.where(c < CHUNK, r >= c, True).astype(BF16)


def _log2_one_minus_sigmoid(z2):
    return jnp.log(1.0 + jnp.exp2(-jnp.abs(z2))) * (-LOG2E) - jnp.maximum(z2, 0.0)


def _split2(x):
    hi = x.astype(BF16)
    return hi, (x - hi.astype(F32)).astype(BF16)


def _sb_prompt_body(q_ref, kt_ref, v_ref, btile_ref, w2_ref, o_ref, kbd_ref, vbd_ref, carry_ref, acc_ref):
    i = pl.program_id(1)
    tq = CHUNK
    npairs, nblk = kbd_ref.shape[0], kbd_ref.shape[1]
    low = lax.broadcasted_iota(jnp.int32, (tq, LANES), 1) < HEAD_DIM

    @pl.when(i == 0)
    def _():
        zero_k = jnp.zeros((HEAD_DIM, tq), BF16)
        for pr in range(npairs):
            rows = slice(pr * LANES, (pr + 1) * LANES)
            for j in range(nblk):
                keys = slice(j * tq, (j + 1) * tq)
                k2 = kt_ref[0, rows, keys].astype(BF16)
                kbd_ref[pr, j, 0:HEAD_DIM, :] = jnp.concatenate([k2[:HEAD_DIM], zero_k], axis=1)
                kbd_ref[pr, j, HEAD_DIM:LANES, :] = jnp.concatenate([zero_k, k2[HEAD_DIM:]], axis=1)
                kbd_ref[pr, j, LANES:, :] = btile_ref[pr]
                v2 = v_ref[0, keys, rows]
                vbd_ref[pr, j, 0:tq, :] = jnp.where(low, v2, 0.0).astype(BF16)
                vbd_ref[pr, j, tq:, :] = jnp.where(low, 0.0, v2).astype(BF16)

    lane = lax.broadcasted_iota(jnp.int32, (tq, LANES), 1)
    bias_cols = jnp.where(lane < 3, 1.0, 0.0)
    qx = [jnp.concatenate([q_ref[0, :, pr * LANES:(pr + 1) * LANES], bias_cols], axis=1).astype(BF16)
          for pr in range(npairs)]
    causal = (lax.broadcasted_iota(jnp.int32, (tq, tq), 1) < lax.broadcasted_iota(jnp.int32, (tq, tq), 0))
    causal2 = jnp.concatenate([causal, causal], axis=1)
    w2 = w2_ref[...]

    def tile(jbs, first):
        zs, parts = [], []
        for pr in range(npairs):
            for jb in jbs:
                zz = _dot(qx[pr], kbd_ref[pr, jb])
                l = _log2_one_minus_sigmoid(zz)
                if first:
                    l = jnp.where(causal2, l, 0.0)
                hi, lo = _split2(l)
                for h in range(2):
                    cols = slice(h * tq, (h + 1) * tq)
                    zs.append(zz[:, cols])
                    parts.append(jnp.concatenate([hi[:, cols], lo[:, cols]], axis=1))
        r = _dot(jnp.concatenate(parts, axis=0), w2)
        per_pair = 2 * len(jbs)
        for pr in range(npairs):
            ws = []
            cs = [None, None] if first else [carry_ref[2 * pr], carry_ref[2 * pr + 1]]
            for n in range(pr * per_pair, (pr + 1) * per_pair):
                h = n % 2
                rr = r[n * tq:(n + 1) * tq]
                suf, tot = rr[:, :tq], rr[:, tq:]
                if first:
                    ws.append(jnp.where(causal, jnp.exp2(zs[n] + suf), 0.0))
                    cs[h] = tot
                else:
                    ws.append(jnp.exp2(zs[n] + suf + cs[h]))
                    cs[h] = cs[h] + tot
            carry_ref[2 * pr] = cs[0]
            carry_ref[2 * pr + 1] = cs[1]
            lhs = jnp.concatenate(ws, axis=1).astype(BF16)
            rhs = jnp.concatenate([vbd_ref[pr, jb] for jb in jbs], axis=0)
            if first:
                acc_ref[pr] = _dot(lhs, rhs)
            else:
                acc_ref[pr] += _dot(lhs, rhs)

    tile([i], True)
    odd = i % 2
    two = (i // 2) % 2

    @pl.when(odd == 1)
    def _():
        tile([i - 1], False)

    @pl.when(two == 1)
    def _():
        top = i - 1 - odd
        tile([top, top - 1], False)

    def body(j, carry):
        top = i - 1 - odd - 2 * two - 4 * j
        tile([top - n for n in range(4)], False)
        return carry

    lax.fori_loop(0, i // 4, body, 0)
    o_ref[0] = jnp.concatenate([acc_ref[pr] for pr in range(npairs)], axis=1)


def _sb_prompt(q, kt, v, btile, w2):
    b, l, sbw = q.shape
    tq = CHUNK
    npairs = sbw // LANES
    qspec = pl.BlockSpec((1, tq, sbw), lambda bi, i: (bi, i, 0))
    return pl.pallas_call(
        _sb_prompt_body,
        grid=(b, l // tq),
        in_specs=[qspec, pl.BlockSpec((1, sbw, l), lambda bi, i: (bi, 0, 0)),
                  pl.BlockSpec((1, l, sbw), lambda bi, i: (bi, 0, 0)),
                  _const_spec(btile.shape), _const_spec(w2.shape)],
        out_specs=qspec,
        out_shape=jax.ShapeDtypeStruct((b, l, sbw), F32),
        scratch_shapes=[pltpu.VMEM((npairs, l // tq, 2 * LANES, 2 * tq), BF16),
                        pltpu.VMEM((npairs, l // tq, 2 * tq, LANES), BF16),
                        pltpu.VMEM((2 * npairs, tq, LANES), F32),
                        pltpu.VMEM((npairs, tq, LANES), F32)],
        compiler_params=_cparams(("arbitrary", "arbitrary")),
        name="sb_prompt",
    )(q, kt, v, btile, w2)


def _sb_sample_body(pt_ref, q_ref, kn_ref, vn_ref, bias_ref, w2_ref, *rest, pages, heads, steps):
    kp = rest[:pages]
    vp = rest[pages:2 * pages]
    o_ref, carry_ref, acc_ref = rest[2 * pages:]
    p = pl.program_id(1)
    nt, width = q_ref.shape[1], q_ref.shape[2]
    rows = nt * heads
    q = q_ref[0]
    qrep = jnp.concatenate([jnp.broadcast_to(q[t:t + 1], (heads, width)) for t in range(nt)], axis=0)
    own = (lax.broadcasted_iota(jnp.int32, (rows, width), 1) // HEAD_DIM
           == lax.broadcasted_iota(jnp.int32, (rows, width), 0) % heads)
    qbd = jnp.where(own, qrep, 0.0)
    bias = bias_ref[...]

    @pl.when(p == 0)
    def _():
        kn = kn_ref[0]
        vn = vn_ref[0]
        t_row = lax.broadcasted_iota(jnp.int32, (rows, 1), 0) // heads
        zs, ls = [], []
        for s in range(nt):
            z2 = jnp.sum(qbd * kn[s:s + 1], axis=1, keepdims=True) + bias[:, :1]
            zs.append(z2)
            ls.append(jnp.where(s < t_row, _log2_one_minus_sigmoid(z2), 0.0))
        o = jnp.zeros((rows, width), F32)
        suffix = jnp.zeros((rows, 1), F32)
        for s in reversed(range(nt)):
            w = jnp.where(s < t_row, jnp.exp2(zs[s] + ls[s] + suffix), 0.0)
            o = o + w * vn[s:s + 1]
            suffix = suffix + ls[s]
        carry_ref[...] = jnp.broadcast_to(suffix, (rows, CHUNK))
        acc_ref[...] = o

    w2 = w2_ref[...]
    qb = qbd.astype(BF16)
    carry = carry_ref[...]
    acc = acc_ref[...]
    zs, parts = [], []
    for g in range(0, pages, 2):
        kk = jnp.concatenate([kp[g][0], kp[g + 1][0]], axis=1).astype(BF16)
        zz = _dot(qb, kk) + bias
        hi, lo = _split2(_log2_one_minus_sigmoid(zz))
        for h in range(2):
            cols = slice(h * CHUNK, (h + 1) * CHUNK)
            zs.append(zz[:, cols])
            parts.append(jnp.concatenate([hi[:, cols], lo[:, cols]], axis=1))
    r = _dot(jnp.concatenate(parts, axis=0), w2)
    ws = []
    for g in range(pages):
        rr = r[g * rows:(g + 1) * rows]
        ws.append(jnp.exp2(zs[g] + rr[:, :CHUNK] + carry))
        carry = carry + rr[:, CHUNK:]
    for g in range(0, pages, 2):
        vv = jnp.concatenate([vp[g][0], vp[g + 1][0]], axis=1).astype(BF16)
        acc = acc + _dot_nt(jnp.concatenate(ws[g:g + 2], axis=1).astype(BF16), vv)
    carry_ref[...] = carry
    acc_ref[...] = acc

    @pl.when(p == steps - 1)
    def _():
        mine = jnp.where(own, acc, 0.0)
        o_ref[0] = jnp.concatenate(
            [jnp.sum(mine[t * heads:(t + 1) * heads], axis=0, keepdims=True) for t in range(nt)], axis=0)


def _sb_sample(q, kn, vn, bias_rows, w2, cache_kt, cache_vt, page_table, *, heads, pages):
    bd, nt, width = q.shape
    rows = nt * heads
    n_pages = page_table.shape[1]
    steps = n_pages // pages
    page = cache_kt.shape[2]

    def page_spec(g):
        def imap(b, p, pt):
            return (pt[b, n_pages - 1 - (p * pages + g)], 0, 0)
        return pl.BlockSpec((1, width, page), imap)

    seq = pl.BlockSpec((1, nt, width), lambda b, p, pt: (b, 0, 0))
    grid_spec = pltpu.PrefetchScalarGridSpec(
        num_scalar_prefetch=1,
        grid=(bd, steps),
        in_specs=[seq, seq, seq, _const_spec(bias_rows.shape), _const_spec(w2.shape)]
        + [page_spec(g) for g in range(pages)] * 2,
        out_specs=seq,
        scratch_shapes=[pltpu.VMEM((rows, CHUNK), F32), pltpu.VMEM((rows, width), F32)],
    )
    return pl.pallas_call(
        functools.partial(_sb_sample_body, pages=pages, heads=heads, steps=steps),
        grid_spec=grid_spec,
        out_shape=jax.ShapeDtypeStruct((bd, nt, width), F32),
        compiler_params=_cparams(("arbitrary", "arbitrary")),
        name="sb_sample",
    )(page_table, q, kn, vn, bias_rows, w2, *([cache_kt] * pages), *([cache_vt] * pages))


def _ssd_body(*refs, cl, nv, has_state, heads, ssmw):
    if has_state:
        (xbc_ref, z_ref, dt_ref, cprev_ref, h0_ref, wconv_ref, bconv_ref, dtb_ref, alog_ref,
         dskip_ref, gssm_ref, y_ref, conv_ref, ssm_ref, ht_ref, xstage, zstage, dtstage) = refs
    else:
        (xbc_ref, z_ref, dt_ref, wconv_ref, bconv_ref, dtb_ref, alog_ref,
         dskip_ref, gssm_ref, y_ref, conv_ref, ssm_ref, ht_ref, xprev) = refs
    c = pl.program_id(1)
    last = pl.num_programs(1) - 1
    taps, convd = wconv_ref.shape
    ncol_s = ssmw // LANES
    hpg = heads // SSM_GROUPS
    ntile = cl // SUBLANES
    assert nv >= taps - 1 and D_STATE == LANES and 2 * HEAD_DIM == LANES
    time_of = lambda p: ntile * (p % SUBLANES) + p // SUBLANES
    row_of = lambda t: SUBLANES * (t % ntile) + t // ntile
    sub = lax.broadcasted_iota(jnp.int32, (SUBLANES, LANES), 0)
    reach = sorted({(a - k) % ntile for a in range(ntile) for k in range(1, taps) if a - k < 0})

    def staged(ref, stage):
        stage[...] = jnp.zeros_like(stage)
        for j in range(stage.shape[0]):
            stage[j, 0:nv, :] = ref[0, :, j * LANES:(j + 1) * LANES]
        return stage

    def tiles(src, j):
        return [src[j, pl.ds(a, SUBLANES, stride=ntile), :] for a in range(ntile)]

    if has_state:
        xsrc, zsrc, dsrc = staged(xbc_ref, xstage), staged(z_ref, zstage), staged(dt_ref, dtstage)
        ht_ref[...] = h0_ref[0].T
    else:
        xsrc, zsrc, dsrc = xbc_ref, z_ref, dt_ref

        @pl.when(c == 0)
        def _():
            xprev[...] = jnp.zeros_like(xprev)
            ht_ref[...] = jnp.zeros_like(ht_ref)

    def conv_col(j):
        lanes = slice(j * LANES, (j + 1) * LANES)
        cur = tiles(xsrc, j)
        if has_state:
            prev = {}
            for a2 in reach:
                tile = jnp.zeros((SUBLANES, LANES), F32)
                for s in range(SUBLANES):
                    t = ntile * (s - SUBLANES) + a2
                    if t >= -(taps - 1):
                        row = cprev_ref[0, SUBLANES + t:SUBLANES + t + 1, lanes]
                        tile = jnp.where(sub == s, jnp.broadcast_to(row, (SUBLANES, LANES)), tile)
                prev[a2] = tile
        else:
            prev = {a2: xprev[n, :, lanes] for n, a2 in enumerate(reach)}
        w = [wconv_ref[taps - 1 - k:taps - k, lanes] for k in range(taps)]
        pieces = []
        for a in range(ntile):
            acc = bconv_ref[:, lanes] + cur[a] * w[0]
            for k in range(1, taps):
                a2, r = (a - k) % ntile, -((a - k) // ntile)
                if r == 0:
                    sh = cur[a2]
                else:
                    sh = jnp.where(sub < r, pltpu.roll(prev[a2], r, axis=0), pltpu.roll(cur[a2], r, axis=0))
                acc = acc + sh * w[k]
            pieces.append(acc)
        conv_ref[0, :, lanes] = jnp.concatenate(
            [cur[t % ntile][t // ntile:t // ntile + 1] for t in range(nv - (taps - 1), nv)], axis=0)
        if not has_state:
            for n, a2 in enumerate(reach):
                xprev[n, :, lanes] = cur[a2]
        return _silu(jnp.concatenate(pieces, axis=0))

    x = jnp.concatenate(tiles(dsrc, 0), axis=0) + dtb_ref[...]
    dt = jnp.maximum(x, 0.0) + jnp.log1p(jnp.exp(-jnp.abs(x)))
    if nv < cl:
        dt = jnp.where(time_of(lax.broadcasted_iota(jnp.int32, (cl, LANES), 0)) < nv, dt, 0.0)
    a = -jnp.exp(alog_ref[...])
    da = dt * a
    causal = (time_of(lax.broadcasted_iota(jnp.int32, (cl, cl), 1))
              <= time_of(lax.broadcasted_iota(jnp.int32, (cl, cl), 0)))
    ltri = jnp.where(causal, 1.0, 0.0).astype(BF16)
    acum = sum(_dot(ltri, part) for part in _split3(da))
    acum = acum * LOG2E
    acum_last = acum[cl - 1:cl, :]
    eac = jnp.exp2(acum)
    wgt = dt * jnp.exp2(acum_last - acum)
    cdec = jnp.exp2(acum_last)
    acum_t = acum.T
    src_t = acum_t - jnp.log2(dt.T)
    wgt_t = wgt.T

    low = lax.broadcasted_iota(jnp.int32, (cl, LANES), 1) < HEAD_DIM
    low1 = low[:1]
    low_n = lax.broadcasted_iota(jnp.int32, (D_STATE, LANES), 1) < HEAD_DIM

    ycols = []
    for g in range(SSM_GROUPS):
        bc = conv_col(ncol_s + g)
        cc = conv_col(ncol_s + SSM_GROUPS + g)
        cb = _dot_nt(cc.astype(BF16), bc.astype(BF16))
        bc_t = bc.T
        for jp in range(hpg // 2):
            h0 = g * hpg + 2 * jp
            col = h0 // 2
            lanes = slice(col * LANES, (col + 1) * LANES)
            xs_p = conv_col(col)
            xs_pb = xs_p.astype(BF16)
            ht_p = ht_ref[:, lanes]
            ht_pb = ht_p.astype(BF16)
            yh, sh = [], []
            for hh in (h0, h0 + 1):
                seg = acum[:, hh:hh + 1] - src_t[hh:hh + 1, :]
                m = (cb * jnp.where(causal, jnp.exp2(seg), 0.0)).astype(BF16)
                off = (cc * eac[:, hh:hh + 1]).astype(BF16)
                if cl == LANES:
                    yh.append(_dot(jnp.concatenate([m, off], axis=1), jnp.concatenate([xs_pb, ht_pb], axis=0)))
                else:
                    yh.append(_dot(m, xs_pb) + _dot(off, ht_pb))
                sh.append(_dot((bc_t * wgt_t[hh:hh + 1, :]).astype(BF16), xs_pb))
            cd = jnp.where(low1, cdec[:, h0:h0 + 1], cdec[:, h0 + 1:h0 + 2])
            ht_ref[:, lanes] = cd * ht_p + jnp.where(low_n, sh[0], sh[1])
            zin = jnp.concatenate(tiles(zsrc, col), axis=0)
            ycols.append((jnp.where(low, yh[0], yh[1]) + dskip_ref[:, lanes] * xs_p) * _silu(zin))

    per_group = ncol_s // SSM_GROUPS
    for g in range(SSM_GROUPS):
        group = ycols[g * per_group:(g + 1) * per_group]
        ms = sum(jnp.sum(yc * yc, axis=-1, keepdims=True) for yc in group) * (1.0 / (per_group * LANES))
        scale = lax.rsqrt(ms + EPS)
        for n, yc in enumerate(group):
            col = g * per_group + n
            lanes = slice(col * LANES, (col + 1) * LANES)
            out = yc * scale * gssm_ref[:, lanes]
            if nv < cl:
                y_ref[0, :, lanes] = jnp.concatenate([out[row_of(t):row_of(t) + 1] for t in range(nv)], axis=0)
            else:
                for a in range(ntile):
                    y_ref[col, pl.ds(a, SUBLANES, stride=ntile), :] = out[a * SUBLANES:(a + 1) * SUBLANES]

    @pl.when(c == last)
    def _():
        ssm_ref[0] = ht_ref[...].T


def _ssd(xbc, z, dt, params, *, heads, cl, batch, state=None):
    has_state = state is not None
    l = dt.shape[1]
    taps, convd = params[0].shape
    ssmw = heads * HEAD_DIM
    per_seq = lambda r, w: pl.BlockSpec((1, r, w), lambda bi, c: (bi, 0, 0))
    const = lambda a: _const_spec(a.shape)
    scratch = [pltpu.VMEM((D_STATE, ssmw), F32)]
    if has_state:
        nv, nchunks = l, 1
        blk = lambda w: pl.BlockSpec((1, nv, w), lambda bi, c: (bi, 0, 0))
        y_shape = (batch, l, ssmw)
        in_specs = [blk(convd), blk(ssmw), blk(LANES), per_seq(SUBLANES, convd), per_seq(ssmw, D_STATE)]
        args = [xbc, z, dt] + list(state)
        scratch += [pltpu.VMEM((w // LANES, cl, LANES), F32) for w in (convd, ssmw, LANES)]
    else:
        nv, nchunks = cl, l // cl
        blk = lambda w: pl.BlockSpec((w // LANES, cl, LANES), lambda bi, c: (0, bi * nchunks + c, 0))
        y_shape = (ssmw // LANES, batch * l, LANES)
        in_specs = [blk(convd), blk(ssmw), pl.BlockSpec((1, cl, LANES), lambda bi, c: (bi, c, 0))]
        args = [xbc, z, dt]
        scratch += [pltpu.VMEM((min(taps - 1, cl // SUBLANES), SUBLANES, convd), F32)]
    in_specs += [const(p) for p in params]
    args += list(params)
    return pl.pallas_call(
        functools.partial(_ssd_body, cl=cl, nv=nv, has_state=has_state, heads=heads, ssmw=ssmw),
        grid=(batch, nchunks),
        in_specs=in_specs,
        out_specs=[blk(ssmw), per_seq(taps - 1, convd), per_seq(ssmw, D_STATE)],
        out_shape=[jax.ShapeDtypeStruct(y_shape, F32),
                   jax.ShapeDtypeStruct((batch, taps - 1, convd), F32),
                   jax.ShapeDtypeStruct((batch, ssmw, D_STATE), F32)],
        scratch_shapes=scratch,
        compiler_params=_cparams(("arbitrary", "arbitrary")),
        name="ssd_sample" if has_state else "ssd_prompt",
    )(*args)


def _outproj_body(o_ref, y_ref, x_ref, gsb_ref, wsb_ref, wy_ref, x1_ref):
    o = _rms(o_ref[...], gsb_ref[...]).astype(BF16)
    if len(y_ref.shape) == 3:
        y = jnp.concatenate([y_ref[j] for j in range(y_ref.shape[0])], axis=1)
    else:
        y = y_ref[...]
    x1_ref[...] = x_ref[...] + _dot(o, wsb_ref[...]) + _dot(y.astype(BF16), wy_ref[...])


def _outproj(o_sb, y, x2d, gsb, wsb, wy, *, tm):
    t, d = x2d.shape
    row = lambda w: pl.BlockSpec((tm, w), lambda i: (i, 0))
    y_spec = row(y.shape[1]) if y.ndim == 2 else pl.BlockSpec((y.shape[0], tm, LANES), lambda i: (0, i, 0))
    return pl.pallas_call(
        _outproj_body,
        grid=(t // tm,),
        in_specs=[row(o_sb.shape[1]), y_spec, row(d), _const_spec(gsb.shape),
                  _const_spec(wsb.shape), _const_spec(wy.shape)],
        out_specs=row(d),
        out_shape=jax.ShapeDtypeStruct((t, d), F32),
        compiler_params=_cparams(("arbitrary",)),
        name="outproj",
    )(o_sb, y, x2d, gsb, wsb, wy)


def _ffn_tail(x1, gate, g1, g2, val, wc_ref, bc_ref, wdown_ref, gfin_ref):
    gc = bc_ref[...] + wc_ref[2:3, :] * gate + wc_ref[1:2, :] * g1 + wc_ref[0:1, :] * g2
    u = (_silu(gc) * val).astype(BF16)
    return _rms(x1 + _dot(u, wdown_ref[...]), gfin_ref[...])


def _ffn_prompt_body(x1_ref, gffn_ref, wup_ref, wc_ref, bc_ref, wdown_ref, gfin_ref,
                     y_ref, fnew_ref, gbuf, *, tm, dff):
    j = pl.program_id(1)

    @pl.when(j == 0)
    def _():
        gbuf[0:SUBLANES, :] = jnp.zeros((SUBLANES, dff), F32)

    x1 = x1_ref[0]
    up = _dot(_rms(x1, gffn_ref[...]).astype(BF16), wup_ref[...])
    gate = up[:, :dff]
    gbuf[SUBLANES:SUBLANES + tm, :] = gate
    g1 = gbuf[SUBLANES - 1:SUBLANES - 1 + tm, :]
    g2 = gbuf[SUBLANES - 2:SUBLANES - 2 + tm, :]
    y_ref[0] = _ffn_tail(x1, gate, g1, g2, up[:, dff:], wc_ref, bc_ref, wdown_ref, gfin_ref)
    gbuf[0:SUBLANES, :] = gbuf[tm:tm + SUBLANES, :]

    @pl.when(j == pl.num_programs(1) - 1)
    def _():
        fnew_ref[0] = gbuf[SUBLANES - 2:SUBLANES, :]


def _ffn_prompt(x1, gffn, wup, wc, bc, wdown, gfin, *, tm):
    b, l, d = x1.shape
    dff = wdown.shape[0]
    blk = pl.BlockSpec((1, tm, d), lambda bi, j: (bi, j, 0))
    const = lambda a: _const_spec(a.shape)
    return pl.pallas_call(
        functools.partial(_ffn_prompt_body, tm=tm, dff=dff),
        grid=(b, l // tm),
        in_specs=[blk] + [const(a) for a in (gffn, wup, wc, bc, wdown, gfin)],
        out_specs=[blk, pl.BlockSpec((1, 2, dff), lambda bi, j: (bi, 0, 0))],
        out_shape=[jax.ShapeDtypeStruct((b, l, d), F32), jax.ShapeDtypeStruct((b, 2, dff), F32)],
        scratch_shapes=[pltpu.VMEM((tm + SUBLANES, dff), F32)],
        compiler_params=_cparams(("arbitrary", "arbitrary")),
        name="ffn_prompt",
    )(x1, gffn, wup, wc, bc, wdown, gfin)


def _ffn_sample_body(x1_ref, p1_ref, p2_ref, gffn_ref, wup_ref, wc_ref, bc_ref, wdown_ref, gfin_ref,
                     y_ref, gate_ref, *, seq, dff):
    x1 = x1_ref[...]
    up = _dot(_rms(x1, gffn_ref[...]).astype(BF16), wup_ref[...])
    gate = up[:, :dff]
    gate_ref[...] = gate
    pos = lax.broadcasted_iota(jnp.int32, (x1.shape[0], 1), 0) % seq
    g1 = jnp.where(pos >= 1, pltpu.roll(gate, 1, axis=0), p1_ref[...])
    g2 = jnp.where(pos >= 2, pltpu.roll(gate, 2, axis=0), p2_ref[...])
    y_ref[...] = _ffn_tail(x1, gate, g1, g2, up[:, dff:], wc_ref, bc_ref, wdown_ref, gfin_ref)


def _ffn_sample(x1, p1, p2, gffn, wup, wc, bc, wdown, gfin, *, seq, tm):
    t, d = x1.shape
    dff = wdown.shape[0]
    row = lambda w: pl.BlockSpec((tm, w), lambda i: (i, 0))
    return pl.pallas_call(
        functools.partial(_ffn_sample_body, seq=seq, dff=dff),
        grid=(t // tm,),
        in_specs=[row(d), row(dff), row(dff)] + [_const_spec(a.shape) for a in (gffn, wup, wc, bc, wdown, gfin)],
        out_specs=[row(d), row(dff)],
        out_shape=[jax.ShapeDtypeStruct((t, d), F32), jax.ShapeDtypeStruct((t, dff), F32)],
        compiler_params=_cparams(("arbitrary",)),
        name="ffn_sample",
    )(x1, p1, p2, gffn, wup, wc, bc, wdown, gfin)


def _pad_lanes(a):
    return jnp.pad(a, ((0, 0), (0, LANES - a.shape[1])))


def kernel(x_prompt, x_sample, cache_k, cache_v, state_ssm, state_conv, state_ffn_conv, page_table,
           g_mix, w_in, sb_bias, g_sb_out, w_conv, b_conv, dt_bias, a_log, d_skip, g_ssm, w_out,
           g_ffn, w_up, w_ffn_conv, b_ffn_conv, w_down, g_final):
    assert g_mix.shape[0] == 1, "single-layer step"
    bp, lp, d = x_prompt.shape
    bd, ld, _ = x_sample.shape
    n_pool, page, _, sb_heads, _ = cache_k.shape
    n_pages = page_table.shape[1]
    assert page == CHUNK
    sbw = sb_heads * HEAD_DIM
    heads = a_log.shape[1]
    ssmw = heads * HEAD_DIM
    convd = w_conv.shape[2]
    dff = w_down.shape[1]

    w = w_in[0]
    wqkv = w[:, :3 * sbw].astype(BF16)
    wz = w[:, 3 * sbw:3 * sbw + ssmw].astype(BF16)
    wxbc = w[:, 3 * sbw + ssmw:3 * sbw + ssmw + convd].astype(BF16)
    wdt = _pad_lanes(w[:, 3 * sbw + ssmw + convd:]).astype(BF16)
    wsb = w_out[0, :sbw].astype(BF16)
    wy = w_out[0, sbw:].astype(BF16)
    wup = w_up[0].astype(BF16)
    wdown = w_down[0].astype(BF16)
    ssd_params = (w_conv[0], b_conv, _pad_lanes(dt_bias), _pad_lanes(a_log),
                  jnp.repeat(d_skip, HEAD_DIM, axis=1), g_ssm)
    w2 = _suffix_matrix()
    bias2 = sb_bias[0] * LOG2E
    npairs = sbw // LANES
    parts = jnp.stack(_split3(bias2)).reshape(3, npairs, 2, 1)
    btile = jnp.zeros((npairs, LANES, 2 * CHUNK), BF16).at[:, :3].set(
        jnp.broadcast_to(parts, (3, npairs, 2, CHUNK)).reshape(3, npairs, 2 * CHUNK).transpose(1, 0, 2))

    wkvt = w[:, sbw:3 * sbw].T.astype(BF16)

    def mix_in(x2d, tm, seq_len=None):
        return _inproj(x2d, g_mix, wqkv, wz, wxbc, wdt, wkvt, sbw=sbw, tm=min(tm, x2d.shape[0]),
                       seq_len=seq_len)

    xp2 = x_prompt.reshape(bp * lp, d)
    q, v_tok, z, xbc, dt, kt_p, vt_p = mix_in(xp2, 256, seq_len=lp)
    to_seq = lambda a: a.reshape(bp, lp, a.shape[-1])
    o_sb = _sb_prompt(to_seq(q), kt_p, to_seq(v_tok), btile, w2)
    y, conv_p, ssm_p = _ssd(xbc, z, to_seq(dt), ssd_params, heads=heads, cl=CHUNK, batch=bp)
    x1 = _outproj(o_sb.reshape(bp * lp, sbw), y, xp2, g_sb_out, wsb, wy, tm=512)
    y_prompt, ffn_p = _ffn_prompt(x1.reshape(bp, lp, d), g_ffn, wup, w_ffn_conv[0], b_ffn_conv, wdown,
                                  g_final[None], tm=256)

    ts = bd * ld
    xs2 = x_sample.reshape(ts, d)
    q, v_s, z, xbc, dt, k_s = mix_in(xs2, 256)
    to_seq = lambda a: a.reshape(bd, ld, a.shape[-1])
    bias_rows = jnp.broadcast_to(jnp.tile(bias2, ld)[:, None], (ld * sb_heads, 2 * CHUNK))
    pool_t = lambda c: jnp.transpose(c[:, :, 0], (0, 2, 3, 1)).reshape(n_pool, sbw, page)
    o_rows = _sb_sample(to_seq(q), to_seq(k_s), to_seq(v_s), bias_rows, w2, pool_t(cache_k), pool_t(cache_v),
                        page_table, heads=sb_heads, pages=max(g for g in (2, 4, 8, 16, 32) if n_pages % g == 0))
    cl_s = -(-ld // BF16_ROWS) * BF16_ROWS
    y, conv_s, ssm_s = _ssd(to_seq(xbc), to_seq(z), to_seq(dt), ssd_params, heads=heads, cl=cl_s, batch=bd,
                            state=(jnp.pad(state_conv[:, 0], ((0, 0), (SUBLANES - state_conv.shape[2], 0), (0, 0))),
                                   state_ssm.reshape(bd, ssmw, D_STATE)))
    x1 = _outproj(o_rows.reshape(ts, sbw), y.reshape(ts, ssmw), xs2, g_sb_out, wsb, wy, tm=ts)
    prev = state_ffn_conv[:, 0]
    p1 = jnp.pad(prev[:, 1:2], ((0, 0), (0, ld - 1), (0, 0))).reshape(ts, dff)
    p2 = jnp.pad(prev, ((0, 0), (0, ld - 2), (0, 0))).reshape(ts, dff)
    y_sample, gate = _ffn_sample(x1, p1, p2, g_ffn, wup, w_ffn_conv[0], b_ffn_conv, wdown, g_final[None],
                                 seq=ld, tm=ts)
    ffn_s = gate.reshape(bd, ld, dff)[:, ld - 2:]

    kv_p = lambda a: jnp.transpose(a.reshape(bp, 1, sb_heads, HEAD_DIM, lp), (0, 4, 1, 2, 3))
    kv_s = lambda a: a.reshape(bd, ld, 1, sb_heads, HEAD_DIM)
    return (y_prompt, y_sample.reshape(bd, ld, d),
            kv_p(kt_p), kv_p(vt_p), ssm_p.reshape(bp, 1, heads, HEAD_DIM, D_STATE),
            conv_p[:, None], ffn_p[:, None],
            kv_s(k_s), kv_s(v_s), ssm_s.reshape(bd, 1, heads, HEAD_DIM, D_STATE),
            conv_s[:, None], ffn_s[:, None])
```

```python
import functools

import jax
import jax.numpy as jnp
from jax import lax
from jax.experimental import pallas as pl
from jax.experimental.pallas import tpu as pltpu

F32 = jnp.float32
BF16 = jnp.bfloat16

HEAD_DIM = 64
D_STATE = 128
SSM_GROUPS = 4
EPS = 1e-6
LOG2E = 1.4426950408889634
LANES = 128
SUBLANES = 8
BF16_ROWS = 16
CHUNK = 128
STEP_BLOCKS = 4
VMEM_LIMIT = 56 * 1024 * 1024

def _cparams(sem):
    return pltpu.CompilerParams(dimension_semantics=sem, vmem_limit_bytes=VMEM_LIMIT)


def _const_spec(shape):
    nd = len(shape)
    return pl.BlockSpec(shape, lambda *_: (0,) * nd, pipeline_mode=pl.Buffered(1))


def _rms(x, g):
    ms = jnp.mean(x * x, axis=-1, keepdims=True)
    return x * lax.rsqrt(ms + EPS) * g


def _silu(x):
    return x / (1.0 + jnp.exp(-x))


def _dot(a, b):
    return jnp.dot(a, b, preferred_element_type=F32)


def _dot_nt(a, b):
    return lax.dot_general(a, b, (((1,), (1,)), ((), ())), preferred_element_type=F32)


def _split3(x):
    hi = x.astype(BF16)
    r1 = x - hi.astype(F32)
    mid = r1.astype(BF16)
    lo = (r1 - mid.astype(F32)).astype(BF16)
    return hi, mid, lo


def _inproj_body(x_ref, g_ref, wqkv_ref, wz_ref, wxbc_ref, wdt_ref, *rest, sbw, scale, transposed_kv):
    h = _rms(x_ref[...], g_ref[...]).astype(BF16)
    qkv = _dot(h, wqkv_ref[...])
    if transposed_kv:
        wkvt_ref, q_ref, v_ref, z_ref, xbc_ref, dt_ref, kt_ref, vt_ref = rest
        kvt = _dot_nt(wkvt_ref[...], h)
        kt_ref[0] = kvt[:sbw]
        vt_ref[0] = kvt[sbw:]
    else:
        q_ref, v_ref, z_ref, xbc_ref, dt_ref, k_ref = rest
        k_ref[...] = qkv[:, sbw:2 * sbw]
    q_ref[...] = qkv[:, :sbw] * scale
    v_ref[...] = qkv[:, 2 * sbw:]
    for ref, w_ref in ((z_ref, wz_ref), (xbc_ref, wxbc_ref)):
        res = _dot(h, w_ref[...])
        if transposed_kv:
            for j in range(ref.shape[0]):
                ref[j] = res[:, j * LANES:(j + 1) * LANES]
        else:
            ref[...] = res
    dt_ref[...] = _dot(h, wdt_ref[...])


def _inproj(x2d, g, wqkv, wz, wxbc, wdt, wkvt, *, sbw, tm, seq_len=None):
    t, d = x2d.shape
    ssmw, convd = wz.shape[1], wxbc.shape[1]
    row = lambda w: pl.BlockSpec((tm, w), lambda i: (i, 0))
    in_specs = [row(d), _const_spec((1, d)), _const_spec(wqkv.shape), _const_spec(wz.shape),
                _const_spec(wxbc.shape), _const_spec(wdt.shape)]
    args = [x2d, g, wqkv, wz, wxbc, wdt]
    out_specs = [row(sbw), row(sbw), row(ssmw), row(convd), row(LANES)]
    out_shape = [jax.ShapeDtypeStruct((t, w), F32) for w in (sbw, sbw, ssmw, convd, LANES)]
    if seq_len is not None:
        for n, w in ((2, ssmw), (3, convd)):
            out_specs[n] = pl.BlockSpec((w // LANES, tm, LANES), lambda i: (0, i, 0))
            out_shape[n] = jax.ShapeDtypeStruct((w // LANES, t, LANES), F32)
    if seq_len is None:
        out_specs.append(row(sbw))
        out_shape.append(jax.ShapeDtypeStruct((t, sbw), F32))
    else:
        nb = seq_len // tm
        in_specs.append(_const_spec(wkvt.shape))
        args.append(wkvt)
        tr = pl.BlockSpec((1, sbw, tm), lambda i: (i // nb, 0, i % nb))
        out_specs += [tr, tr]
        out_shape += [jax.ShapeDtypeStruct((t // seq_len, sbw, seq_len), F32)] * 2
    return pl.pallas_call(
        functools.partial(_inproj_body, sbw=sbw, scale=HEAD_DIM ** -0.5 * LOG2E, transposed_kv=seq_len is not None),
        grid=(t // tm,),
        in_specs=in_specs,
        out_specs=out_specs,
        out_shape=out_shape,
        compiler_params=_cparams(("arbitrary",)),
        name="inproj",
    )(*args)


def _suffix_matrix():
    r = jnp.arange(2 * CHUNK)[:, None] % CHUNK
    c = jnp.arange(2 * CHUNK)[None, :]
    return jnp.where(c < CHUNK, r >= c, True).astype(BF16)


def _log2_one_minus_sigmoid(z2):
    return jnp.log(1.0 + jnp.exp2(-jnp.abs(z2))) * (-LOG2E) - jnp.maximum(z2, 0.0)


def _split2(x):
    hi = x.astype(BF16)
    return hi, (x - hi.astype(F32)).astype(BF16)


def _sb_prompt_body(q_ref, kt_ref, v_ref, btile_ref, w2_ref, o_ref, kbd_ref, vbd_ref, carry_ref, acc_ref):
    i = pl.program_id(1)
    tq = CHUNK
    npairs, nblk = kbd_ref.shape[0], kbd_ref.shape[1]
    low = lax.broadcasted_iota(jnp.int32, (tq, LANES), 1) < HEAD_DIM

    @pl.when(i == 0)
    def _():
        zero_k = jnp.zeros((HEAD_DIM, tq), BF16)
        for pr in range(npairs):
            rows = slice(pr * LANES, (pr + 1) * LANES)
            for j in range(nblk):
                keys = slice(j * tq, (j + 1) * tq)
                k2 = kt_ref[0, rows, keys].astype(BF16)
                kbd_ref[pr, j, 0:HEAD_DIM, :] = jnp.concatenate([k2[:HEAD_DIM], zero_k], axis=1)
                kbd_ref[pr, j, HEAD_DIM:LANES, :] = jnp.concatenate([zero_k, k2[HEAD_DIM:]], axis=1)
                kbd_ref[pr, j, LANES:, :] = btile_ref[pr]
                v2 = v_ref[0, keys, rows]
                vbd_ref[pr, j, 0:tq, :] = jnp.where(low, v2, 0.0).astype(BF16)
                vbd_ref[pr, j, tq:, :] = jnp.where(low, 0.0, v2).astype(BF16)

    lane = lax.broadcasted_iota(jnp.int32, (tq, LANES), 1)
    bias_cols = jnp.where(lane < 3, 1.0, 0.0)
    qx = [jnp.concatenate([q_ref[0, :, pr * LANES:(pr + 1) * LANES], bias_cols], axis=1).astype(BF16)
          for pr in range(npairs)]
    causal = (lax.broadcasted_iota(jnp.int32, (tq, tq), 1) < lax.broadcasted_iota(jnp.int32, (tq, tq), 0))
    causal2 = jnp.concatenate([causal, causal], axis=1)
    w2 = w2_ref[...]

    def tile(jbs, first):
        zs, parts = [], []
        for pr in range(npairs):
            for nb, jb in enumerate(jbs):
                zz = _dot(qx[pr], kbd_ref[pr, jb])
                l = _log2_one_minus_sigmoid(zz)
                if first and nb == 0:
                    l = jnp.where(causal2, l, 0.0)
                hi, lo = _split2(l)
                for h in range(2):
                    cols = slice(h * tq, (h + 1) * tq)
                    zs.append(zz[:, cols])
                    parts.append(jnp.concatenate([hi[:, cols], lo[:, cols]], axis=1))
        r = _dot(jnp.concatenate(parts, axis=0), w2)
        per_pair = 2 * len(jbs)
        for pr in range(npairs):
            ws = []
            cs = [None, None] if first else [carry_ref[2 * pr], carry_ref[2 * pr + 1]]
            for n in range(pr * per_pair, (pr + 1) * per_pair):
                h = n % 2
                rr = r[n * tq:(n + 1) * tq]
                suf, tot = rr[:, :tq], rr[:, tq:]
                if cs[h] is None:
                    ws.append(jnp.where(causal, jnp.exp2(zs[n] + suf), 0.0))
                    cs[h] = tot
                else:
                    ws.append(jnp.exp2(zs[n] + suf + cs[h]))
                    cs[h] = cs[h] + tot
            carry_ref[2 * pr] = cs[0]
            carry_ref[2 * pr + 1] = cs[1]
            lhs = jnp.concatenate(ws, axis=1).astype(BF16)
            rhs = jnp.concatenate([vbd_ref[pr, jb] for jb in jbs], axis=0)
            if first:
                acc_ref[pr] = _dot(lhs, rhs)
            else:
                acc_ref[pr] += _dot(lhs, rhs)

    for n_first in range(1, STEP_BLOCKS + 1):
        @pl.when(i % STEP_BLOCKS == n_first - 1)
        def _(n_first=n_first):
            tile([i - n for n in range(n_first)], True)

    def body(j, carry):
        top = i - (i % STEP_BLOCKS + 1) - STEP_BLOCKS * j
        tile([top - n for n in range(STEP_BLOCKS)], False)
        return carry

    lax.fori_loop(0, i // STEP_BLOCKS, body, 0)
    o_ref[0] = jnp.concatenate([acc_ref[pr] for pr in range(npairs)], axis=1)


def _sb_prompt(q, kt, v, btile, w2):
    b, l, sbw = q.shape
    tq = CHUNK
    npairs = sbw // LANES
    qspec = pl.BlockSpec((1, tq, sbw), lambda bi, i: (bi, i, 0))
    return pl.pallas_call(
        _sb_prompt_body,
        grid=(b, l // tq),
        in_specs=[qspec, pl.BlockSpec((1, sbw, l), lambda bi, i: (bi, 0, 0)),
                  pl.BlockSpec((1, l, sbw), lambda bi, i: (bi, 0, 0)),
                  _const_spec(btile.shape), _const_spec(w2.shape)],
        out_specs=qspec,
        out_shape=jax.ShapeDtypeStruct((b, l, sbw), F32),
        scratch_shapes=[pltpu.VMEM((npairs, l // tq, 2 * LANES, 2 * tq), BF16),
                        pltpu.VMEM((npairs, l // tq, 2 * tq, LANES), BF16),
                        pltpu.VMEM((2 * npairs, tq, LANES), F32),
                        pltpu.VMEM((npairs, tq, LANES), F32)],
        compiler_params=_cparams(("arbitrary", "arbitrary")),
        name="sb_prompt",
    )(q, kt, v, btile, w2)


def _sb_sample_body(pt_ref, q_ref, kn_ref, vn_ref, bias_ref, w2_ref, *rest, pages, heads, steps):
    kp = rest[:pages]
    vp = rest[pages:2 * pages]
    o_ref, carry_ref, acc_ref = rest[2 * pages:]
    p = pl.program_id(1)
    nt, width = q_ref.shape[1], q_ref.shape[2]
    rows = nt * heads
    q = q_ref[0]
    qrep = jnp.concatenate([jnp.broadcast_to(q[t:t + 1], (heads, width)) for t in range(nt)], axis=0)
    own = (lax.broadcasted_iota(jnp.int32, (rows, width), 1) // HEAD_DIM
           == lax.broadcasted_iota(jnp.int32, (rows, width), 0) % heads)
    qbd = jnp.where(own, qrep, 0.0)
    bias = bias_ref[...]

    @pl.when(p == 0)
    def _():
        kn = kn_ref[0]
        vn = vn_ref[0]
        t_row = lax.broadcasted_iota(jnp.int32, (rows, 1), 0) // heads
        zs, ls = [], []
        for s in range(nt):
            z2 = jnp.sum(qbd * kn[s:s + 1], axis=1, keepdims=True) + bias[:, :1]
            zs.append(z2)
            ls.append(jnp.where(s < t_row, _log2_one_minus_sigmoid(z2), 0.0))
        o = jnp.zeros((rows, width), F32)
        suffix = jnp.zeros((rows, 1), F32)
        for s in reversed(range(nt)):
            w = jnp.where(s < t_row, jnp.exp2(zs[s] + ls[s] + suffix), 0.0)
            o = o + w * vn[s:s + 1]
            suffix = suffix + ls[s]
        carry_ref[...] = jnp.broadcast_to(suffix, (rows, CHUNK))
        acc_ref[...] = o

    w2 = w2_ref[...]
    qb = qbd.astype(BF16)
    carry = carry_ref[...]
    acc = acc_ref[...]
    zs, parts = [], []
    for g in range(0, pages, 2):
        kk = jnp.concatenate([kp[g][0], kp[g + 1][0]], axis=1).astype(BF16)
        zz = _dot(qb, kk) + bias
        hi, lo = _split2(_log2_one_minus_sigmoid(zz))
        for h in range(2):
            cols = slice(h * CHUNK, (h + 1) * CHUNK)
            zs.append(zz[:, cols])
            parts.append(jnp.concatenate([hi[:, cols], lo[:, cols]], axis=1))
    r = _dot(jnp.concatenate(parts, axis=0), w2)
    ws = []
    for g in range(pages):
        rr = r[g * rows:(g + 1) * rows]
        ws.append(jnp.exp2(zs[g] + rr[:, :CHUNK] + carry))
        carry = carry + rr[:, CHUNK:]
    for g in range(0, pages, 2):
        vv = jnp.concatenate([vp[g][0], vp[g + 1][0]], axis=1).astype(BF16)
        acc = acc + _dot_nt(jnp.concatenate(ws[g:g + 2], axis=1).astype(BF16), vv)
    carry_ref[...] = carry
    acc_ref[...] = acc

    @pl.when(p == steps - 1)
    def _():
        mine = jnp.where(own, acc, 0.0)
        o_ref[0] = jnp.concatenate(
            [jnp.sum(mine[t * heads:(t + 1) * heads], axis=0, keepdims=True) for t in range(nt)], axis=0)


def _sb_sample(q, kn, vn, bias_rows, w2, cache_kt, cache_vt, page_table, *, heads, pages):
    bd, nt, width = q.shape
    rows = nt * heads
    n_pages = page_table.shape[1]
    steps = n_pages // pages
    page = cache_kt.shape[2]

    def page_spec(g):
        def imap(b, p, pt):
            return (pt[b, n_pages - 1 - (p * pages + g)], 0, 0)
        return pl.BlockSpec((1, width, page), imap)

    seq = pl.BlockSpec((1, nt, width), lambda b, p, pt: (b, 0, 0))
    grid_spec = pltpu.PrefetchScalarGridSpec(
        num_scalar_prefetch=1,
        grid=(bd, steps),
        in_specs=[seq, seq, seq, _const_spec(bias_rows.shape), _const_spec(w2.shape)]
        + [page_spec(g) for g in range(pages)] * 2,
        out_specs=seq,
        scratch_shapes=[pltpu.VMEM((rows, CHUNK), F32), pltpu.VMEM((rows, width), F32)],
    )
    return pl.pallas_call(
        functools.partial(_sb_sample_body, pages=pages, heads=heads, steps=steps),
        grid_spec=grid_spec,
        out_shape=jax.ShapeDtypeStruct((bd, nt, width), F32),
        compiler_params=_cparams(("arbitrary", "arbitrary")),
        name="sb_sample",
    )(page_table, q, kn, vn, bias_rows, w2, *([cache_kt] * pages), *([cache_vt] * pages))


def _ssd_body(*refs, cl, nv, has_state, heads, ssmw):
    if has_state:
        (xbc_ref, z_ref, dt_ref, cprev_ref, h0_ref, wconv_ref, bconv_ref, dtb_ref, alog_ref,
         dskip_ref, gssm_ref, y_ref, conv_ref, ssm_ref, xstage, zstage, dtstage) = refs
    else:
        (xbc_ref, z_ref, dt_ref, wconv_ref, bconv_ref, dtb_ref, alog_ref,
         dskip_ref, gssm_ref, y_ref, conv_ref, ssm_ref, ht_ref, xprev) = refs
    c = pl.program_id(1)
    last = pl.num_programs(1) - 1
    taps, convd = wconv_ref.shape
    ncol_s = ssmw // LANES
    hpg = heads // SSM_GROUPS
    ntile = cl // SUBLANES
    assert nv >= taps - 1 and D_STATE == LANES and 2 * HEAD_DIM == LANES
    time_of = lambda p: ntile * (p % SUBLANES) + p // SUBLANES
    row_of = lambda t: SUBLANES * (t % ntile) + t // ntile
    sub = lax.broadcasted_iota(jnp.int32, (SUBLANES, LANES), 0)
    reach = sorted({(a - k) % ntile for a in range(ntile) for k in range(1, taps) if a - k < 0})

    def staged(ref, stage):
        stage[...] = jnp.zeros_like(stage)
        for j in range(stage.shape[0]):
            stage[j, 0:nv, :] = ref[0, :, j * LANES:(j + 1) * LANES]
        return stage

    def tiles(src, j):
        return [src[j, pl.ds(a, SUBLANES, stride=ntile), :] for a in range(ntile)]

    if has_state:
        xsrc, zsrc, dsrc = staged(xbc_ref, xstage), staged(z_ref, zstage), staged(dt_ref, dtstage)
    else:
        xsrc, zsrc, dsrc = xbc_ref, z_ref, dt_ref

        @pl.when(c == 0)
        def _():
            xprev[...] = jnp.zeros_like(xprev)
            ht_ref[...] = jnp.zeros_like(ht_ref)

    def conv_col(j):
        lanes = slice(j * LANES, (j + 1) * LANES)
        cur = tiles(xsrc, j)
        if has_state:
            prev = {}
            for a2 in reach:
                tile = jnp.zeros((SUBLANES, LANES), F32)
                for s in range(SUBLANES):
                    t = ntile * (s - SUBLANES) + a2
                    if t >= -(taps - 1):
                        row = cprev_ref[0, SUBLANES + t:SUBLANES + t + 1, lanes]
                        tile = jnp.where(sub == s, jnp.broadcast_to(row, (SUBLANES, LANES)), tile)
                prev[a2] = tile
        else:
            prev = {a2: xprev[n, :, lanes] for n, a2 in enumerate(reach)}
        w = [wconv_ref[taps - 1 - k:taps - k, lanes] for k in range(taps)]
        pieces = []
        for a in range(ntile):
            acc = bconv_ref[:, lanes] + cur[a] * w[0]
            for k in range(1, taps):
                a2, r = (a - k) % ntile, -((a - k) // ntile)
                if r == 0:
                    sh = cur[a2]
                else:
                    sh = jnp.where(sub < r, pltpu.roll(prev[a2], r, axis=0), pltpu.roll(cur[a2], r, axis=0))
                acc = acc + sh * w[k]
            pieces.append(acc)
        conv_ref[0, :, lanes] = jnp.concatenate(
            [cur[t % ntile][t // ntile:t // ntile + 1] for t in range(nv - (taps - 1), nv)], axis=0)
        if not has_state:
            for n, a2 in enumerate(reach):
                xprev[n, :, lanes] = cur[a2]
        return _silu(jnp.concatenate(pieces, axis=0))

    x = jnp.concatenate(tiles(dsrc, 0), axis=0) + dtb_ref[...]
    dt = jnp.maximum(x, 0.0) + jnp.log1p(jnp.exp(-jnp.abs(x)))
    if nv < cl:
        dt = jnp.where(time_of(lax.broadcasted_iota(jnp.int32, (cl, LANES), 0)) < nv, dt, 0.0)
    a = -jnp.exp(alog_ref[...])
    da = dt * a
    causal = (time_of(lax.broadcasted_iota(jnp.int32, (cl, cl), 1))
              <= time_of(lax.broadcasted_iota(jnp.int32, (cl, cl), 0)))
    ltri = jnp.where(causal, 1.0, 0.0).astype(BF16)
    acum = sum(_dot(ltri, part) for part in _split3(da))
    acum = acum * LOG2E
    acum_last = acum[cl - 1:cl, :]
    eac = jnp.exp2(acum)
    wgt = dt * jnp.exp2(acum_last - acum)
    cdec = jnp.exp2(acum_last)
    acum_t = acum.T
    src_t = acum_t - jnp.log2(dt.T)

    low = lax.broadcasted_iota(jnp.int32, (cl, LANES), 1) < HEAD_DIM
    low1 = low[:1]
    low_rows = lax.broadcasted_iota(jnp.int32, (LANES, D_STATE), 0) < HEAD_DIM

    ycols = []
    for g in range(SSM_GROUPS):
        bc = conv_col(ncol_s + g)
        cc = conv_col(ncol_s + SSM_GROUPS + g)
        bc_b = bc.astype(BF16)
        cb = _dot_nt(cc.astype(BF16), bc_b)
        if not has_state:
            bc_tb = bc.T.astype(BF16)
        for jp in range(hpg // 2):
            h0 = g * hpg + 2 * jp
            col = h0 // 2
            lanes = slice(col * LANES, (col + 1) * LANES)
            xs_p = conv_col(col)
            xs_pb = xs_p.astype(BF16)
            st = h0_ref[0, lanes, :] if has_state else ht_ref[:, lanes]
            st_b = st.astype(BF16)
            yh = []
            for hh in (h0, h0 + 1):
                seg = acum[:, hh:hh + 1] - src_t[hh:hh + 1, :]
                m = (cb * jnp.where(causal, jnp.exp2(seg), 0.0)).astype(BF16)
                off = (cc * eac[:, hh:hh + 1]).astype(BF16)
                if has_state:
                    yh.append(_dot(m, xs_pb) + _dot_nt(off, st_b))
                else:
                    yh.append(_dot(jnp.concatenate([m, off], axis=1), jnp.concatenate([xs_pb, st_b], axis=0)))
            xs_w = (xs_p * jnp.where(low, wgt[:, h0:h0 + 1], wgt[:, h0 + 1:h0 + 2])).astype(BF16)
            if has_state:
                s_new = lax.dot_general(xs_w, bc_b, (((0,), (0,)), ((), ())), preferred_element_type=F32)
                ssm_ref[0, lanes, :] = jnp.where(low_rows, cdec[:, h0:h0 + 1], cdec[:, h0 + 1:h0 + 2]) * st + s_new
            else:
                cd = jnp.where(low1, cdec[:, h0:h0 + 1], cdec[:, h0 + 1:h0 + 2])
                ht_ref[:, lanes] = cd * st + _dot(bc_tb, xs_w)
            zin = jnp.concatenate(tiles(zsrc, col), axis=0)
            ycols.append((jnp.where(low, yh[0], yh[1]) + dskip_ref[:, lanes] * xs_p) * _silu(zin))

    per_group = ncol_s // SSM_GROUPS
    for g in range(SSM_GROUPS):
        group = ycols[g * per_group:(g + 1) * per_group]
        ms = sum(jnp.sum(yc * yc, axis=-1, keepdims=True) for yc in group) * (1.0 / (per_group * LANES))
        scale = lax.rsqrt(ms + EPS)
        for n, yc in enumerate(group):
            col = g * per_group + n
            lanes = slice(col * LANES, (col + 1) * LANES)
            out = yc * scale * gssm_ref[:, lanes]
            if nv < cl:
                y_ref[0, :, lanes] = jnp.concatenate([out[row_of(t):row_of(t) + 1] for t in range(nv)], axis=0)
            else:
                for a in range(ntile):
                    y_ref[col, pl.ds(a, SUBLANES, stride=ntile), :] = out[a * SUBLANES:(a + 1) * SUBLANES]

    if not has_state:
        @pl.when(c == last)
        def _():
            ssm_ref[0] = ht_ref[...].T


def _ssd(xbc, z, dt, params, *, heads, cl, batch, state=None):
    has_state = state is not None
    l = dt.shape[1]
    taps, convd = params[0].shape
    ssmw = heads * HEAD_DIM
    per_seq = lambda r, w: pl.BlockSpec((1, r, w), lambda bi, c: (bi, 0, 0))
    const = lambda a: _const_spec(a.shape)
    scratch = []
    if has_state:
        nv, nchunks = l, 1
        blk = lambda w: pl.BlockSpec((1, nv, w), lambda bi, c: (bi, 0, 0))
        y_shape = (batch, l, ssmw)
        in_specs = [blk(convd), blk(ssmw), blk(LANES), per_seq(SUBLANES, convd), per_seq(ssmw, D_STATE)]
        args = [xbc, z, dt] + list(state)
        scratch += [pltpu.VMEM((w // LANES, cl, LANES), F32) for w in (convd, ssmw, LANES)]
    else:
        nv, nchunks = cl, l // cl
        blk = lambda w: pl.BlockSpec((w // LANES, cl, LANES), lambda bi, c: (0, bi * nchunks + c, 0))
        y_shape = (ssmw // LANES, batch * l, LANES)
        in_specs = [blk(convd), blk(ssmw), pl.BlockSpec((1, cl, LANES), lambda bi, c: (bi, c, 0))]
        args = [xbc, z, dt]
        scratch += [pltpu.VMEM((D_STATE, ssmw), F32),
                    pltpu.VMEM((min(taps - 1, cl // SUBLANES), SUBLANES, convd), F32)]
    in_specs += [const(p) for p in params]
    args += list(params)
    return pl.pallas_call(
        functools.partial(_ssd_body, cl=cl, nv=nv, has_state=has_state, heads=heads, ssmw=ssmw),
        grid=(batch, nchunks),
        in_specs=in_specs,
        out_specs=[blk(ssmw), per_seq(taps - 1, convd), per_seq(ssmw, D_STATE)],
        out_shape=[jax.ShapeDtypeStruct(y_shape, F32),
                   jax.ShapeDtypeStruct((batch, taps - 1, convd), F32),
                   jax.ShapeDtypeStruct((batch, ssmw, D_STATE), F32)],
        scratch_shapes=scratch,
        compiler_params=_cparams(("arbitrary", "arbitrary")),
        name="ssd_sample" if has_state else "ssd_prompt",
    )(*args)


def _mix_out(o_sb, y_ref, x, gsb_ref, wsb_ref, wy_ref):
    o = _rms(o_sb, gsb_ref[...]).astype(BF16)
    if len(y_ref.shape) == 3:
        y = jnp.concatenate([y_ref[j] for j in range(y_ref.shape[0])], axis=1)
    else:
        y = y_ref[...]
    return x + _dot(o, wsb_ref[...]) + _dot(y.astype(BF16), wy_ref[...])


def _outproj_body(o_ref, y_ref, x_ref, gsb_ref, wsb_ref, wy_ref, x1_ref):
    x1_ref[...] = _mix_out(o_ref[...], y_ref, x_ref[...], gsb_ref, wsb_ref, wy_ref)


def _outproj(o_sb, y, x2d, gsb, wsb, wy, *, tm):
    t, d = x2d.shape
    row = lambda w: pl.BlockSpec((tm, w), lambda i: (i, 0))
    y_spec = row(y.shape[1]) if y.ndim == 2 else pl.BlockSpec((y.shape[0], tm, LANES), lambda i: (0, i, 0))
    return pl.pallas_call(
        _outproj_body,
        grid=(t // tm,),
        in_specs=[row(o_sb.shape[1]), y_spec, row(d), _const_spec(gsb.shape),
                  _const_spec(wsb.shape), _const_spec(wy.shape)],
        out_specs=row(d),
        out_shape=jax.ShapeDtypeStruct((t, d), F32),
        compiler_params=_cparams(("arbitrary",)),
        name="outproj",
    )(o_sb, y, x2d, gsb, wsb, wy)


def _ffn_tail(x1, gate, g1, g2, val, wc_ref, bc_ref, wdown_ref, gfin_ref):
    gc = bc_ref[...] + wc_ref[2:3, :] * gate + wc_ref[1:2, :] * g1 + wc_ref[0:1, :] * g2
    u = (_silu(gc) * val).astype(BF16)
    return _rms(x1 + _dot(u, wdown_ref[...]), gfin_ref[...])


def _ffn_prompt_body(o_ref, ymix_ref, x_ref, gsb_ref, wsb_ref, wy_ref, gffn_ref, wup_ref, wc_ref, bc_ref,
                     wdown_ref, gfin_ref, y_ref, fnew_ref, gbuf, *, tm, dff):
    j = pl.program_id(1)

    @pl.when(j == 0)
    def _():
        gbuf[0:SUBLANES, :] = jnp.zeros((SUBLANES, dff), F32)

    x1 = _mix_out(o_ref[0], ymix_ref, x_ref[0], gsb_ref, wsb_ref, wy_ref)
    up =_dot(_rms(x1, gffn_ref[...]).astype(BF16), wup_ref[...])
    gate = up[:, :dff]
    gbuf[SUBLANES:SUBLANES + tm, :] = gate
    g1 = gbuf[SUBLANES - 1:SUBLANES - 1 + tm, :]
    g2 = gbuf[SUBLANES - 2:SUBLANES - 2 + tm, :]
    y_ref[0] = _ffn_tail(x1, gate, g1, g2, up[:, dff:], wc_ref, bc_ref, wdown_ref, gfin_ref)
    gbuf[0:SUBLANES, :] = gbuf[tm:tm + SUBLANES, :]

    @pl.when(j == pl.num_programs(1) - 1)
    def _():
        fnew_ref[0] = gbuf[SUBLANES - 2:SUBLANES, :]


def _ffn_prompt(o_sb, y_t, x, mix_w, gffn, wup, wc, bc, wdown, gfin, *, tm):
    b, l, d = x.shape
    dff = wdown.shape[0]
    nb = l // tm
    blk = pl.BlockSpec((1, tm, d), lambda bi, j: (bi, j, 0))
    const = lambda a: _const_spec(a.shape)
    consts = tuple(mix_w) + (gffn, wup, wc, bc, wdown, gfin)
    return pl.pallas_call(
        functools.partial(_ffn_prompt_body, tm=tm, dff=dff),
        grid=(b, nb),
        in_specs=[pl.BlockSpec((1, tm, o_sb.shape[2]), lambda bi, j: (bi, j, 0)),
                  pl.BlockSpec((y_t.shape[0], tm, LANES), lambda bi, j: (0, bi * nb + j, 0)),
                  blk] + [const(a) for a in consts],
        out_specs=[blk, pl.BlockSpec((1, 2, dff), lambda bi, j: (bi, 0, 0))],
        out_shape=[jax.ShapeDtypeStruct((b, l, d), F32), jax.ShapeDtypeStruct((b, 2, dff), F32)],
        scratch_shapes=[pltpu.VMEM((tm + SUBLANES, dff), F32)],
        compiler_params=_cparams(("arbitrary", "arbitrary")),
        name="ffn_prompt",
    )(o_sb, y_t, x, *consts)


def _ffn_sample_body(x1_ref, p1_ref, p2_ref, gffn_ref, wup_ref, wc_ref, bc_ref, wdown_ref, gfin_ref,
                     y_ref, gate_ref, *, seq, dff):
    x1 = x1_ref[...]
    up = _dot(_rms(x1, gffn_ref[...]).astype(BF16), wup_ref[...])
    gate = up[:, :dff]
    gate_ref[...] = gate
    pos = lax.broadcasted_iota(jnp.int32, (x1.shape[0], 1), 0) % seq
    g1 = jnp.where(pos >= 1, pltpu.roll(gate, 1, axis=0), p1_ref[...])
    g2 = jnp.where(pos >= 2, pltpu.roll(gate, 2, axis=0), p2_ref[...])
    y_ref[...] = _ffn_tail(x1, gate, g1, g2, up[:, dff:], wc_ref, bc_ref, wdown_ref, gfin_ref)


def _ffn_sample(x1, p1, p2, gffn, wup, wc, bc, wdown, gfin, *, seq, tm):
    t, d = x1.shape
    dff = wdown.shape[0]
    row = lambda w: pl.BlockSpec((tm, w), lambda i: (i, 0))
    return pl.pallas_call(
        functools.partial(_ffn_sample_body, seq=seq, dff=dff),
        grid=(t // tm,),
        in_specs=[row(d), row(dff), row(dff)] + [_const_spec(a.shape) for a in (gffn, wup, wc, bc, wdown, gfin)],
        out_specs=[row(d), row(dff)],
        out_shape=[jax.ShapeDtypeStruct((t, d), F32), jax.ShapeDtypeStruct((t, dff), F32)],
        compiler_params=_cparams(("arbitrary",)),
        name="ffn_sample",
    )(x1, p1, p2, gffn, wup, wc, bc, wdown, gfin)


def _pad_lanes(a):
    return jnp.pad(a, ((0, 0), (0, LANES - a.shape[1])))


def kernel(x_prompt, x_sample, cache_k, cache_v, state_ssm, state_conv, state_ffn_conv, page_table,
           g_mix, w_in, sb_bias, g_sb_out, w_conv, b_conv, dt_bias, a_log, d_skip, g_ssm, w_out,
           g_ffn, w_up, w_ffn_conv, b_ffn_conv, w_down, g_final):
    assert g_mix.shape[0] == 1, "single-layer step"
    bp, lp, d = x_prompt.shape
    bd, ld, _ = x_sample.shape
    n_pool, page, _, sb_heads, _ = cache_k.shape
    n_pages = page_table.shape[1]
    assert page == CHUNK
    sbw = sb_heads * HEAD_DIM
    heads = a_log.shape[1]
    ssmw = heads * HEAD_DIM
    convd = w_conv.shape[2]
    dff = w_down.shape[1]

    w = w_in[0]
    wqkv = w[:, :3 * sbw].astype(BF16)
    wz = w[:, 3 * sbw:3 * sbw + ssmw].astype(BF16)
    wxbc = w[:, 3 * sbw + ssmw:3 * sbw + ssmw + convd].astype(BF16)
    wdt = _pad_lanes(w[:, 3 * sbw + ssmw + convd:]).astype(BF16)
    wsb = w_out[0, :sbw].astype(BF16)
    wy = w_out[0, sbw:].astype(BF16)
    wup = w_up[0].astype(BF16)
    wdown = w_down[0].astype(BF16)
    ssd_params = (w_conv[0], b_conv, _pad_lanes(dt_bias), _pad_lanes(a_log),
                  jnp.repeat(d_skip, HEAD_DIM, axis=1), g_ssm)
    w2 = _suffix_matrix()
    bias2 = sb_bias[0] * LOG2E
    npairs = sbw // LANES
    parts = jnp.stack(_split3(bias2)).reshape(3, npairs, 2, 1)
    btile = jnp.zeros((npairs, LANES, 2 * CHUNK), BF16).at[:, :3].set(
        jnp.broadcast_to(parts, (3, npairs, 2, CHUNK)).reshape(3, npairs, 2 * CHUNK).transpose(1, 0, 2))

    wkvt = w[:, sbw:3 * sbw].T.astype(BF16)

    def mix_in(x2d, tm, seq_len=None):
        return _inproj(x2d, g_mix, wqkv, wz, wxbc, wdt, wkvt, sbw=sbw, tm=min(tm, x2d.shape[0]),
                       seq_len=seq_len)

    xp2 = x_prompt.reshape(bp * lp, d)
    q, v_tok, z, xbc, dt, kt_p, vt_p = mix_in(xp2, 256, seq_len=lp)
    to_seq = lambda a: a.reshape(bp, lp, a.shape[-1])
    o_sb = _sb_prompt(to_seq(q), kt_p, to_seq(v_tok), btile, w2)
    y, conv_p, ssm_p = _ssd(xbc, z, to_seq(dt), ssd_params, heads=heads, cl=CHUNK, batch=bp)
    y_prompt, ffn_p = _ffn_prompt(o_sb, y, x_prompt, (g_sb_out, wsb, wy), g_ffn, wup, w_ffn_conv[0], b_ffn_conv,
                                  wdown, g_final[None], tm=256)

    ts = bd * ld
    xs2 = x_sample.reshape(ts, d)
    q, v_s, z, xbc, dt, k_s = mix_in(xs2, 256)
    to_seq = lambda a: a.reshape(bd, ld, a.shape[-1])
    bias_rows = jnp.broadcast_to(jnp.tile(bias2, ld)[:, None], (ld * sb_heads, 2 * CHUNK))
    pool_t = lambda c: jnp.transpose(c[:, :, 0], (0, 2, 3, 1)).reshape(n_pool, sbw, page)
    o_rows = _sb_sample(to_seq(q), to_seq(k_s), to_seq(v_s), bias_rows, w2, pool_t(cache_k), pool_t(cache_v),
                        page_table, heads=sb_heads, pages=max(g for g in (2, 4, 8, 16, 32) if n_pages % g == 0))
    cl_s = -(-ld // BF16_ROWS) * BF16_ROWS
    y, conv_s, ssm_s = _ssd(to_seq(xbc), to_seq(z), to_seq(dt), ssd_params, heads=heads, cl=cl_s, batch=bd,
                            state=(jnp.pad(state_conv[:, 0], ((0, 0), (SUBLANES - state_conv.shape[2], 0), (0, 0))),
                                   state_ssm.reshape(bd, ssmw, D_STATE)))
    x1 = _outproj(o_rows.reshape(ts, sbw), y.reshape(ts, ssmw), xs2, g_sb_out, wsb, wy, tm=ts)
    prev = state_ffn_conv[:, 0]
    p1 = jnp.pad(prev[:, 1:2], ((0, 0), (0, ld - 1), (0, 0))).reshape(ts, dff)
    p2 = jnp.pad(prev, ((0, 0), (0, ld - 2), (0, 0))).reshape(ts, dff)
    y_sample, gate = _ffn_sample(x1, p1, p2, g_ffn, wup, w_ffn_conv[0], b_ffn_conv, wdown, g_final[None],
                                 seq=ld, tm=ts)
    ffn_s = gate.reshape(bd, ld, dff)[:, ld - 2:]

    kv_p = lambda a: jnp.transpose(a.reshape(bp, 1, sb_heads, HEAD_DIM, lp), (0, 4, 1, 2, 3))
    kv_s = lambda a: a.reshape(bd, ld, 1, sb_heads, HEAD_DIM)
    return (y_prompt, y_sample.reshape(bd, ld, d),
            kv_p(kt_p), kv_p(vt_p), ssm_p.reshape(bp, 1, heads, HEAD_DIM, D_STATE),
            conv_p[:, None], ffn_p[:, None],
            kv_s(k_s), kv_s(v_s), ssm_s.reshape(bd, 1, heads, HEAD_DIM, D_STATE),
            conv_s[:, None], ffn_s[:, None])
```

```python
import functools

import jax
import jax.numpy as jnp
from jax import lax
from jax.experimental import pallas as pl
from jax.experimental.pallas import tpu as pltpu

F32 = jnp.float32
BF16 = jnp.bfloat16

HEAD_DIM = 64
D_STATE = 128
SSM_GROUPS = 4
EPS = 1e-6
LOG2E = 1.4426950408889634
LANES = 128
SUBLANES = 8
BF16_ROWS = 16
CHUNK = 128
STEP_BLOCKS = 4
VMEM_LIMIT = 56 * 1024 * 1024

def _cparams(sem):
    return pltpu.CompilerParams(dimension_semantics=sem, vmem_limit_bytes=VMEM_LIMIT)


def _const_spec(shape):
    nd = len(shape)
    return pl.BlockSpec(shape, lambda *_: (0,) * nd, pipeline_mode=pl.Buffered(1))


def _rms(x, g):
    ms = jnp.mean(x * x, axis=-1, keepdims=True)
    return x * lax.rsqrt(ms + EPS) * g


def _silu(x):
    return x / (1.0 + jnp.exp(-x))


def _dot(a, b):
    return jnp.dot(a, b, preferred_element_type=F32)


def _dot_nt(a, b):
    return lax.dot_general(a, b, (((1,), (1,)), ((), ())), preferred_element_type=F32)


def _split3(x):
    hi = x.astype(BF16)
    r1 = x - hi.astype(F32)
    mid = r1.astype(BF16)
    lo = (r1 - mid.astype(F32)).astype(BF16)
    return hi, mid, lo


def _inproj_body(x_ref, g_ref, wqkv_ref, wz_ref, wxbc_ref, wdt_ref, *rest, sbw, scale, transposed_kv):
    h = _rms(x_ref[...], g_ref[...]).astype(BF16)
    qkv = _dot(h, wqkv_ref[...])
    if transposed_kv:
        wkvt_ref, q_ref, v_ref, z_ref, xbc_ref, dt_ref, kt_ref, vt_ref = rest
        kvt = _dot_nt(wkvt_ref[...], h)
        kt_ref[0] = kvt[:sbw]
        vt_ref[0] = kvt[sbw:]
    else:
        q_ref, v_ref, z_ref, xbc_ref, dt_ref, k_ref = rest
        k_ref[...] = qkv[:, sbw:2 * sbw]
    q_ref[...] = qkv[:, :sbw] * scale
    v_ref[...] = qkv[:, 2 * sbw:]
    for ref, w_ref in ((z_ref, wz_ref), (xbc_ref, wxbc_ref)):
        res = _dot(h, w_ref[...])
        if transposed_kv:
            for j in range(ref.shape[0]):
                ref[j] = res[:, j * LANES:(j + 1) * LANES]
        else:
            ref[...] = res
    dt_ref[...] = _dot(h, wdt_ref[...])


def _inproj(x2d, g, wqkv, wz, wxbc, wdt, wkvt, *, sbw, tm, seq_len=None):
    t, d = x2d.shape
    ssmw, convd = wz.shape[1], wxbc.shape[1]
    row = lambda w: pl.BlockSpec((tm, w), lambda i: (i, 0))
    in_specs = [row(d), _const_spec((1, d)), _const_spec(wqkv.shape), _const_spec(wz.shape),
                _const_spec(wxbc.shape), _const_spec(wdt.shape)]
    args = [x2d, g, wqkv, wz, wxbc, wdt]
    out_specs = [row(sbw), row(sbw), row(ssmw), row(convd), row(LANES)]
    out_shape = [jax.ShapeDtypeStruct((t, w), F32) for w in (sbw, sbw, ssmw, convd, LANES)]
    if seq_len is not None:
        for n, w in ((2, ssmw), (3, convd)):
            out_specs[n] = pl.BlockSpec((w // LANES, tm, LANES), lambda i: (0, i, 0))
            out_shape[n] = jax.ShapeDtypeStruct((w // LANES, t, LANES), F32)
    if seq_len is None:
        out_specs.append(row(sbw))
        out_shape.append(jax.ShapeDtypeStruct((t, sbw), F32))
    else:
        nb = seq_len // tm
        in_specs.append(_const_spec(wkvt.shape))
        args.append(wkvt)
        tr = pl.BlockSpec((1, sbw, tm), lambda i: (i // nb, 0, i % nb))
        out_specs += [tr, tr]
        out_shape += [jax.ShapeDtypeStruct((t // seq_len, sbw, seq_len), F32)] * 2
    return pl.pallas_call(
        functools.partial(_inproj_body, sbw=sbw, scale=HEAD_DIM ** -0.5 * LOG2E, transposed_kv=seq_len is not None),
        grid=(t // tm,),
        in_specs=in_specs,
        out_specs=out_specs,
        out_shape=out_shape,
        compiler_params=_cparams(("arbitrary",)),
        name="inproj",
    )(*args)


def _suffix_matrix():
    r = jnp.arange(2 * CHUNK)[:, None] % CHUNK
    c = jnp.arange(2 * CHUNK)[None, :]
    return jnp.where(c < CHUNK, r >= c, True).astype(BF16)


def _log2_one_minus_sigmoid(z2):
    return jnp.log(1.0 + jnp.exp2(-jnp.abs(z2))) * (-LOG2E) - jnp.maximum(z2, 0.0)


def _split2(x):
    hi = x.astype(BF16)
    return hi, (x - hi.astype(F32)).astype(BF16)


def _sb_prompt_body(q_ref, kt_ref, v_ref, btile_ref, w2_ref, o_ref, kbd_ref, vbd_ref, carry_ref, acc_ref):
    i = pl.program_id(1)
    tq = CHUNK
    npairs, nblk = kbd_ref.shape[0], kbd_ref.shape[1]
    low = lax.broadcasted_iota(jnp.int32, (tq, LANES), 1) < HEAD_DIM

    @pl.when(i == 0)
    def _():
        zero_k = jnp.zeros((HEAD_DIM, tq), BF16)
        for pr in range(npairs):
            rows = slice(pr * LANES, (pr + 1) * LANES)
            for j in range(nblk):
                keys = slice(j * tq, (j + 1) * tq)
                k2 = kt_ref[0, rows, keys].astype(BF16)
                kbd_ref[pr, j, 0:HEAD_DIM, :] = jnp.concatenate([k2[:HEAD_DIM], zero_k], axis=1)
                kbd_ref[pr, j, HEAD_DIM:LANES, :] = jnp.concatenate([zero_k, k2[HEAD_DIM:]], axis=1)
                kbd_ref[pr, j, LANES:, :] = btile_ref[pr]
                v2 = v_ref[0, keys, rows]
                vbd_ref[pr, j, 0:tq, :] = jnp.where(low, v2, 0.0).astype(BF16)
                vbd_ref[pr, j, tq:, :] = jnp.where(low, 0.0, v2).astype(BF16)

    lane = lax.broadcasted_iota(jnp.int32, (tq, LANES), 1)
    bias_cols = jnp.where(lane < 3, 1.0, 0.0)
    qx = [jnp.concatenate([q_ref[0, :, pr * LANES:(pr + 1) * LANES], bias_cols], axis=1).astype(BF16)
          for pr in range(npairs)]
    causal = (lax.broadcasted_iota(jnp.int32, (tq, tq), 1) < lax.broadcasted_iota(jnp.int32, (tq, tq), 0))
    causal2 = jnp.concatenate([causal, causal], axis=1)
    w2 = w2_ref[...]

    def tile(jbs, first):
        zs, parts = [], []
        for pr in range(npairs):
            for nb, jb in enumerate(jbs):
                zz = _dot(qx[pr], kbd_ref[pr, jb])
                l = _log2_one_minus_sigmoid(zz)
                if first and nb == 0:
                    l = jnp.where(causal2, l, 0.0)
                hi, lo = _split2(l)
                for h in range(2):
                    cols = slice(h * tq, (h + 1) * tq)
                    zs.append(zz[:, cols])
                    parts.append(jnp.concatenate([hi[:, cols], lo[:, cols]], axis=1))
        r = _dot(jnp.concatenate(parts, axis=0), w2)
        per_pair = 2 * len(jbs)
        for pr in range(npairs):
            ws = []
            cs = [None, None] if first else [carry_ref[2 * pr], carry_ref[2 * pr + 1]]
            for n in range(pr * per_pair, (pr + 1) * per_pair):
                h = n % 2
                rr = r[n * tq:(n + 1) * tq]
                suf, tot = rr[:, :tq], rr[:, tq:]
                if cs[h] is None:
                    ws.append(jnp.where(causal, jnp.exp2(zs[n] + suf), 0.0))
                    cs[h] = tot
                else:
                    ws.append(jnp.exp2(zs[n] + suf + cs[h]))
                    cs[h] = cs[h] + tot
            carry_ref[2 * pr] = cs[0]
            carry_ref[2 * pr + 1] = cs[1]
            lhs = jnp.concatenate(ws, axis=1).astype(BF16)
            rhs = jnp.concatenate([vbd_ref[pr, jb] for jb in jbs], axis=0)
            if first:
                acc_ref[pr] = _dot(lhs, rhs)
            else:
                acc_ref[pr] += _dot(lhs, rhs)

    for n_first in range(1, STEP_BLOCKS + 1):
        @pl.when(i % STEP_BLOCKS == n_first - 1)
        def _(n_first=n_first):
            tile([i - n for n in range(n_first)], True)

    def body(j, carry):
        top = i - (i % STEP_BLOCKS + 1) - STEP_BLOCKS * j
        tile([top - n for n in range(STEP_BLOCKS)], False)
        return carry

    lax.fori_loop(0, i // STEP_BLOCKS, body, 0)
    o_ref[0] = jnp.concatenate([acc_ref[pr] for pr in range(npairs)], axis=1)


def _sb_prompt(q, kt, v, btile, w2):
    b, l, sbw = q.shape
    tq = CHUNK
    npairs = sbw // LANES
    qspec = pl.BlockSpec((1, tq, sbw), lambda bi, i: (bi, i, 0))
    return pl.pallas_call(
        _sb_prompt_body,
        grid=(b, l // tq),
        in_specs=[qspec, pl.BlockSpec((1, sbw, l), lambda bi, i: (bi, 0, 0)),
                  pl.BlockSpec((1, l, sbw), lambda bi, i: (bi, 0, 0)),
                  _const_spec(btile.shape), _const_spec(w2.shape)],
        out_specs=qspec,
        out_shape=jax.ShapeDtypeStruct((b, l, sbw), F32),
        scratch_shapes=[pltpu.VMEM((npairs, l // tq, 2 * LANES, 2 * tq), BF16),
                        pltpu.VMEM((npairs, l // tq, 2 * tq, LANES), BF16),
                        pltpu.VMEM((2 * npairs, tq, LANES), F32),
                        pltpu.VMEM((npairs, tq, LANES), F32)],
        compiler_params=_cparams(("arbitrary", "arbitrary")),
        name="sb_prompt",
    )(q, kt, v, btile, w2)


def _sb_sample_body(pt_ref, q_ref, kn_ref, vn_ref, bias_ref, w2_ref, *rest, pages, heads, steps):
    kp = rest[:pages]
    vp = rest[pages:2 * pages]
    o_ref, carry_ref, acc_ref = rest[2 * pages:]
    p = pl.program_id(1)
    nt, width = q_ref.shape[1], q_ref.shape[2]
    rows = nt * heads
    q = q_ref[0]
    qrep = jnp.concatenate([jnp.broadcast_to(q[t:t + 1], (heads, width)) for t in range(nt)], axis=0)
    own = (lax.broadcasted_iota(jnp.int32, (rows, width), 1) // HEAD_DIM
           == lax.broadcasted_iota(jnp.int32, (rows, width), 0) % heads)
    qbd = jnp.where(own, qrep, 0.0)
    bias = bias_ref[...]

    @pl.when(p == 0)
    def _():
        kn = kn_ref[0]
        vn = vn_ref[0]
        t_row = lax.broadcasted_iota(jnp.int32, (rows, 1), 0) // heads
        zs, ls = [], []
        for s in range(nt):
            z2 = jnp.sum(qbd * kn[s:s + 1], axis=1, keepdims=True) + bias[:, :1]
            zs.append(z2)
            ls.append(jnp.where(s < t_row, _log2_one_minus_sigmoid(z2), 0.0))
        o = jnp.zeros((rows, width), F32)
        suffix = jnp.zeros((rows, 1), F32)
        for s in reversed(range(nt)):
            w = jnp.where(s < t_row, jnp.exp2(zs[s] + ls[s] + suffix), 0.0)
            o = o + w * vn[s:s + 1]
            suffix = suffix + ls[s]
        carry_ref[...] = jnp.broadcast_to(suffix, (rows, CHUNK))
        acc_ref[...] = o

    w2 = w2_ref[...]
    qb = qbd.astype(BF16)
    carry = carry_ref[...]
    acc = acc_ref[...]
    zs, parts = [], []
    for g in range(0, pages, 2):
        kk = jnp.concatenate([kp[g][0], kp[g + 1][0]], axis=1).astype(BF16)
        zz = _dot(qb, kk) + bias
        hi, lo = _split2(_log2_one_minus_sigmoid(zz))
        for h in range(2):
            cols = slice(h * CHUNK, (h + 1) * CHUNK)
            zs.append(zz[:, cols])
            parts.append(jnp.concatenate([hi[:, cols], lo[:, cols]], axis=1))
    r = _dot(jnp.concatenate(parts, axis=0), w2)
    ws = []
    for g in range(pages):
        rr = r[g * rows:(g + 1) * rows]
        ws.append(jnp.exp2(zs[g] + rr[:, :CHUNK] + carry))
        carry = carry + rr[:, CHUNK:]
    for g in range(0, pages, 2):
        vv = jnp.concatenate([vp[g][0], vp[g + 1][0]], axis=1).astype(BF16)
        acc = acc + _dot_nt(jnp.concatenate(ws[g:g + 2], axis=1).astype(BF16), vv)
    carry_ref[...] = carry
    acc_ref[...] = acc

    @pl.when(p == steps - 1)
    def _():
        mine = jnp.where(own, acc, 0.0)
        o_ref[0] = jnp.concatenate(
            [jnp.sum(mine[t * heads:(t + 1) * heads], axis=0, keepdims=True) for t in range(nt)], axis=0)


def _sb_sample(q, kn, vn, bias_rows, w2, cache_kt, cache_vt, page_table, *, heads, pages):
    bd, nt, width = q.shape
    rows = nt * heads
    n_pages = page_table.shape[1]
    steps = n_pages // pages
    page = cache_kt.shape[2]

    def page_spec(g):
        def imap(b, p, pt):
            return (pt[b, n_pages - 1 - (p * pages + g)], 0, 0)
        return pl.BlockSpec((1, width, page), imap)

    seq = pl.BlockSpec((1, nt, width), lambda b, p, pt: (b, 0, 0))
    grid_spec = pltpu.PrefetchScalarGridSpec(
        num_scalar_prefetch=1,
        grid=(bd, steps),
        in_specs=[seq, seq, seq, _const_spec(bias_rows.shape), _const_spec(w2.shape)]
        + [page_spec(g) for g in range(pages)] * 2,
        out_specs=seq,
        scratch_shapes=[pltpu.VMEM((rows, CHUNK), F32), pltpu.VMEM((rows, width), F32)],
    )
    return pl.pallas_call(
        functools.partial(_sb_sample_body, pages=pages, heads=heads, steps=steps),
        grid_spec=grid_spec,
        out_shape=jax.ShapeDtypeStruct((bd, nt, width), F32),
        compiler_params=_cparams(("arbitrary", "arbitrary")),
        name="sb_sample",
    )(page_table, q, kn, vn, bias_rows, w2, *([cache_kt] * pages), *([cache_vt] * pages))


def _ssd_body(*refs, cl, nv, has_state, heads, ssmw, nseq=1, row0=0):
    if nseq > 1:
        for sq in range(nseq):
            one = lambda r: r.at[pl.ds(sq, 1)]
            views = (list(refs[:3]) + [one(r) for r in refs[3:5]] + list(refs[5:12])
                     + [one(r) for r in refs[12:14]] + [r.at[sq] for r in refs[14:]])
            _ssd_body(*views, cl=cl, nv=nv, has_state=has_state, heads=heads, ssmw=ssmw, row0=sq * nv)
        return
    if has_state:
        (xbc_ref, z_ref, dt_ref, cprev_ref, h0_ref, wconv_ref, bconv_ref, dtb_ref, alog_ref,
         dskip_ref, gssm_ref, y_ref, conv_ref, ssm_ref, xstage, zstage, dtstage) = refs
    else:
        (xbc_ref, z_ref, dt_ref, wconv_ref, bconv_ref, dtb_ref, alog_ref,
         dskip_ref, gssm_ref, y_ref, conv_ref, ssm_ref, ht_ref, xprev) = refs
    c = pl.program_id(1)
    last = pl.num_programs(1) - 1
    taps, convd = wconv_ref.shape
    ncol_s = ssmw // LANES
    hpg = heads // SSM_GROUPS
    ntile = cl // SUBLANES
    assert nv >= taps - 1 and D_STATE == LANES and 2 * HEAD_DIM == LANES
    time_of = lambda p: ntile * (p % SUBLANES) + p // SUBLANES
    row_of = lambda t: SUBLANES * (t % ntile) + t // ntile
    sub = lax.broadcasted_iota(jnp.int32, (SUBLANES, LANES), 0)
    reach = sorted({(a - k) % ntile for a in range(ntile) for k in range(1, taps) if a - k < 0})

    def staged(ref, stage):
        stage[...] = jnp.zeros_like(stage)
        for j in range(stage.shape[0]):
            stage[j, 0:nv, :] = ref[row0:row0 + nv, j * LANES:(j + 1) * LANES]
        return stage

    def tiles(src, j):
        return [src[j, pl.ds(a, SUBLANES, stride=ntile), :] for a in range(ntile)]

    if has_state:
        xsrc, zsrc, dsrc = staged(xbc_ref, xstage), staged(z_ref, zstage), staged(dt_ref, dtstage)
    else:
        xsrc, zsrc, dsrc = xbc_ref, z_ref, dt_ref

        @pl.when(c == 0)
        def _():
            xprev[...] = jnp.zeros_like(xprev)
            ht_ref[...] = jnp.zeros_like(ht_ref)

    def conv_col(j):
        lanes = slice(j * LANES, (j + 1) * LANES)
        cur = tiles(xsrc, j)
        if has_state:
            prev = {}
            for a2 in reach:
                tile = jnp.zeros((SUBLANES, LANES), F32)
                for s in range(SUBLANES):
                    t = ntile * (s - SUBLANES) + a2
                    if t >= -(taps - 1):
                        row = cprev_ref[0, SUBLANES + t:SUBLANES + t + 1, lanes]
                        tile = jnp.where(sub == s, jnp.broadcast_to(row, (SUBLANES, LANES)), tile)
                prev[a2] = tile
        else:
            prev = {a2: xprev[n, :, lanes] for n, a2 in enumerate(reach)}
        w = [wconv_ref[taps - 1 - k:taps - k, lanes] for k in range(taps)]
        pieces = []
        for a in range(ntile):
            acc = bconv_ref[:, lanes] + cur[a] * w[0]
            for k in range(1, taps):
                a2, r = (a - k) % ntile, -((a - k) // ntile)
                if r == 0:
                    sh = cur[a2]
                else:
                    sh = jnp.where(sub < r, pltpu.roll(prev[a2], r, axis=0), pltpu.roll(cur[a2], r, axis=0))
                acc = acc + sh * w[k]
            pieces.append(acc)
        conv_ref[0, :, lanes] = jnp.concatenate(
            [cur[t % ntile][t // ntile:t // ntile + 1] for t in range(nv - (taps - 1), nv)], axis=0)
        if not has_state:
            for n, a2 in enumerate(reach):
                xprev[n, :, lanes] = cur[a2]
        return _silu(jnp.concatenate(pieces, axis=0))

    x = jnp.concatenate(tiles(dsrc, 0), axis=0) + dtb_ref[...]
    dt = jnp.maximum(x, 0.0) + jnp.log1p(jnp.exp(-jnp.abs(x)))
    if nv < cl:
        dt = jnp.where(time_of(lax.broadcasted_iota(jnp.int32, (cl, LANES), 0)) < nv, dt, 0.0)
    a = -jnp.exp(alog_ref[...])
    da = dt * a
    causal = (time_of(lax.broadcasted_iota(jnp.int32, (cl, cl), 1))
              <= time_of(lax.broadcasted_iota(jnp.int32, (cl, cl), 0)))
    ltri = jnp.where(causal, 1.0, 0.0).astype(BF16)
    acum = sum(_dot(ltri, part) for part in _split3(da))
    acum = acum * LOG2E
    acum_last = acum[cl - 1:cl, :]
    eac = jnp.exp2(acum)
    wgt = dt * jnp.exp2(acum_last - acum)
    cdec = jnp.exp2(acum_last)
    acum_t = acum.T
    src_t = acum_t - jnp.log2(dt.T)

    low = lax.broadcasted_iota(jnp.int32, (cl, LANES), 1) < HEAD_DIM
    low1 = low[:1]
    low_rows = lax.broadcasted_iota(jnp.int32, (LANES, D_STATE), 0) < HEAD_DIM

    ycols = []
    for g in range(SSM_GROUPS):
        bc = conv_col(ncol_s + g)
        cc = conv_col(ncol_s + SSM_GROUPS + g)
        bc_b = bc.astype(BF16)
        cb = _dot_nt(cc.astype(BF16), bc_b)
        if not has_state:
            bc_tb = bc.T.astype(BF16)
        for jp in range(hpg // 2):
            h0 = g * hpg + 2 * jp
            col = h0 // 2
            lanes = slice(col * LANES, (col + 1) * LANES)
            xs_p = conv_col(col)
            xs_pb = xs_p.astype(BF16)
            st = h0_ref[0, lanes, :] if has_state else ht_ref[:, lanes]
            st_b = st.astype(BF16)
            yh = []
            for hh in (h0, h0 + 1):
                seg = acum[:, hh:hh + 1] - src_t[hh:hh + 1, :]
                m = (cb * jnp.where(causal, jnp.exp2(seg), 0.0)).astype(BF16)
                off = (cc * eac[:, hh:hh + 1]).astype(BF16)
                if has_state:
                    yh.append(_dot(m, xs_pb) + _dot_nt(off, st_b))
                else:
                    yh.append(_dot(jnp.concatenate([m, off], axis=1), jnp.concatenate([xs_pb, st_b], axis=0)))
            xs_w = (xs_p * jnp.where(low, wgt[:, h0:h0 + 1], wgt[:, h0 + 1:h0 + 2])).astype(BF16)
            if has_state:
                s_new = lax.dot_general(xs_w, bc_b, (((0,), (0,)), ((), ())), preferred_element_type=F32)
                ssm_ref[0, lanes, :] = jnp.where(low_rows, cdec[:, h0:h0 + 1], cdec[:, h0 + 1:h0 + 2]) * st + s_new
            else:
                cd = jnp.where(low1, cdec[:, h0:h0 + 1], cdec[:, h0 + 1:h0 + 2])
                ht_ref[:, lanes] = cd * st + _dot(bc_tb, xs_w)
            zin = jnp.concatenate(tiles(zsrc, col), axis=0)
            ycols.append((jnp.where(low, yh[0], yh[1]) + dskip_ref[:, lanes] * xs_p) * _silu(zin))

    per_group = ncol_s // SSM_GROUPS
    for g in range(SSM_GROUPS):
        group = ycols[g * per_group:(g + 1) * per_group]
        ms = sum(jnp.sum(yc * yc, axis=-1, keepdims=True) for yc in group) * (1.0 / (per_group * LANES))
        scale = lax.rsqrt(ms + EPS)
        for n, yc in enumerate(group):
            col = g * per_group + n
            lanes = slice(col * LANES, (col + 1) * LANES)
            out = yc * scale * gssm_ref[:, lanes]
            if nv < cl:
                y_ref[row0:row0 + nv, lanes] = jnp.concatenate(
                    [out[row_of(t):row_of(t) + 1] for t in range(nv)], axis=0)
            else:
                for a in range(ntile):
                    y_ref[col, pl.ds(a, SUBLANES, stride=ntile), :] = out[a * SUBLANES:(a + 1) * SUBLANES]

    if not has_state:
        @pl.when(c == last)
        def _():
            ssm_ref[0] = ht_ref[...].T


def _ssd(xbc, z, dt, params, *, heads, cl, batch, seq_len, state=None):
    has_state = state is not None
    l = seq_len
    taps, convd = params[0].shape
    ssmw = heads * HEAD_DIM
    nseq = SUBLANES // l if has_state else 1
    per_seq = lambda r, w: pl.BlockSpec((nseq, r, w), lambda bi, c: (bi, 0, 0))
    const = lambda a: _const_spec(a.shape)
    scratch = []
    if has_state:
        assert nseq * l == SUBLANES and nseq > 1 and batch % nseq == 0
        nv, nchunks = l, 1
        blk = lambda w: pl.BlockSpec((nseq * nv, w), lambda bi, c: (bi, 0))
        y_shape = (batch * l, ssmw)
        in_specs = [blk(convd), blk(ssmw), blk(LANES), per_seq(SUBLANES, convd), per_seq(ssmw, D_STATE)]
        args = [xbc, z, dt] + list(state)
        scratch += [pltpu.VMEM((nseq, w // LANES, cl, LANES), F32) for w in (convd, ssmw, LANES)]
    else:
        nv, nchunks = cl, l // cl
        blk = lambda w: pl.BlockSpec((w // LANES, cl, LANES), lambda bi, c: (0, bi * nchunks + c, 0))
        y_shape = (ssmw // LANES, batch * l, LANES)
        in_specs = [blk(convd), blk(ssmw), pl.BlockSpec((1, cl, LANES), lambda bi, c: (bi, c, 0))]
        args = [xbc, z, dt]
        scratch += [pltpu.VMEM((D_STATE, ssmw), F32),
                    pltpu.VMEM((min(taps - 1, cl // SUBLANES), SUBLANES, convd), F32)]
    in_specs += [const(p) for p in params]
    args += list(params)
    return pl.pallas_call(
        functools.partial(_ssd_body, cl=cl, nv=nv, has_state=has_state, heads=heads, ssmw=ssmw, nseq=nseq),
        grid=(batch // nseq, nchunks),
        in_specs=in_specs,
        out_specs=[blk(ssmw), per_seq(taps - 1, convd), per_seq(ssmw, D_STATE)],
        out_shape=[jax.ShapeDtypeStruct(y_shape, F32),
                   jax.ShapeDtypeStruct((batch, taps - 1, convd), F32),
                   jax.ShapeDtypeStruct((batch, ssmw, D_STATE), F32)],
        scratch_shapes=scratch,
        compiler_params=_cparams(("arbitrary", "arbitrary")),
        name="ssd_sample" if has_state else "ssd_prompt",
    )(*args)


def _mix_out(o_sb, y_ref, x, gsb_ref, wsb_ref, wy_ref):
    o = _rms(o_sb, gsb_ref[...]).astype(BF16)
    if len(y_ref.shape) == 3:
        y = jnp.concatenate([y_ref[j] for j in range(y_ref.shape[0])], axis=1)
    else:
        y = y_ref[...]
    return x + _dot(o, wsb_ref[...]) + _dot(y.astype(BF16), wy_ref[...])


def _outproj_body(o_ref, y_ref, x_ref, gsb_ref, wsb_ref, wy_ref, x1_ref):
    x1_ref[...] = _mix_out(o_ref[...], y_ref, x_ref[...], gsb_ref, wsb_ref, wy_ref)


def _outproj(o_sb, y, x2d, gsb, wsb, wy, *, tm):
    t, d = x2d.shape
    row = lambda w: pl.BlockSpec((tm, w), lambda i: (i, 0))
    y_spec = row(y.shape[1]) if y.ndim == 2 else pl.BlockSpec((y.shape[0], tm, LANES), lambda i: (0, i, 0))
    return pl.pallas_call(
        _outproj_body,
        grid=(t // tm,),
        in_specs=[row(o_sb.shape[1]), y_spec, row(d), _const_spec(gsb.shape),
                  _const_spec(wsb.shape), _const_spec(wy.shape)],
        out_specs=row(d),
        out_shape=jax.ShapeDtypeStruct((t, d), F32),
        compiler_params=_cparams(("arbitrary",)),
        name="outproj",
    )(o_sb, y, x2d, gsb, wsb, wy)


def _ffn_tail(x1, gate, g1, g2, val, wc_ref, bc_ref, wdown_ref, gfin_ref):
    gc = bc_ref[...] + wc_ref[2:3, :] * gate + wc_ref[1:2, :] * g1 + wc_ref[0:1, :] * g2
    u = (_silu(gc) * val).astype(BF16)
    return _rms(x1 + _dot(u, wdown_ref[...]), gfin_ref[...])


def _ffn_prompt_body(o_ref, ymix_ref, x_ref, gsb_ref, wsb_ref, wy_ref, gffn_ref, wup_ref, wc_ref, bc_ref,
                     wdown_ref, gfin_ref, y_ref, fnew_ref, gbuf, *, tm, dff):
    j = pl.program_id(1)

    @pl.when(j == 0)
    def _():
        gbuf[0:SUBLANES, :] = jnp.zeros((SUBLANES, dff), F32)

    x1 = _mix_out(o_ref[0], ymix_ref, x_ref[0], gsb_ref, wsb_ref, wy_ref)
    up =_dot(_rms(x1, gffn_ref[...]).astype(BF16), wup_ref[...])
    gate = up[:, :dff]
    gbuf[SUBLANES:SUBLANES + tm, :] = gate
    g1 = gbuf[SUBLANES - 1:SUBLANES - 1 + tm, :]
    g2 = gbuf[SUBLANES - 2:SUBLANES - 2 + tm, :]
    y_ref[0] = _ffn_tail(x1, gate, g1, g2, up[:, dff:], wc_ref, bc_ref, wdown_ref, gfin_ref)
    gbuf[0:SUBLANES, :] = gbuf[tm:tm + SUBLANES, :]

    @pl.when(j == pl.num_programs(1) - 1)
    def _():
        fnew_ref[0] = gbuf[SUBLANES - 2:SUBLANES, :]


def _ffn_prompt(o_sb, y_t, x, mix_w, gffn, wup, wc, bc, wdown, gfin, *, tm):
    b, l, d = x.shape
    dff = wdown.shape[0]
    nb = l // tm
    blk = pl.BlockSpec((1, tm, d), lambda bi, j: (bi, j, 0))
    const = lambda a: _const_spec(a.shape)
    consts = tuple(mix_w) + (gffn, wup, wc, bc, wdown, gfin)
    return pl.pallas_call(
        functools.partial(_ffn_prompt_body, tm=tm, dff=dff),
        grid=(b, nb),
        in_specs=[pl.BlockSpec((1, tm, o_sb.shape[2]), lambda bi, j: (bi, j, 0)),
                  pl.BlockSpec((y_t.shape[0], tm, LANES), lambda bi, j: (0, bi * nb + j, 0)),
                  blk] + [const(a) for a in consts],
        out_specs=[blk, pl.BlockSpec((1, 2, dff), lambda bi, j: (bi, 0, 0))],
        out_shape=[jax.ShapeDtypeStruct((b, l, d), F32), jax.ShapeDtypeStruct((b, 2, dff), F32)],
        scratch_shapes=[pltpu.VMEM((tm + SUBLANES, dff), F32)],
        compiler_params=_cparams(("arbitrary", "arbitrary")),
        name="ffn_prompt",
    )(o_sb, y_t, x, *consts)


def _ffn_sample_body(x1_ref, p1_ref, p2_ref, gffn_ref, wup_ref, wc_ref, bc_ref, wdown_ref, gfin_ref,
                     y_ref, gate_ref, *, seq, dff):
    x1 = x1_ref[...]
    up = _dot(_rms(x1, gffn_ref[...]).astype(BF16), wup_ref[...])
    gate = up[:, :dff]
    gate_ref[...] = gate
    pos = lax.broadcasted_iota(jnp.int32, (x1.shape[0], 1), 0) % seq
    g1 = jnp.where(pos >= 1, pltpu.roll(gate, 1, axis=0), p1_ref[...])
    g2 = jnp.where(pos >= 2, pltpu.roll(gate, 2, axis=0), p2_ref[...])
    y_ref[...] = _ffn_tail(x1, gate, g1, g2, up[:, dff:], wc_ref, bc_ref, wdown_ref, gfin_ref)


def _ffn_sample(x1, p1, p2, gffn, wup, wc, bc, wdown, gfin, *, seq, tm):
    t, d = x1.shape
    dff = wdown.shape[0]
    row = lambda w: pl.BlockSpec((tm, w), lambda i: (i, 0))
    return pl.pallas_call(
        functools.partial(_ffn_sample_body, seq=seq, dff=dff),
        grid=(t // tm,),
        in_specs=[row(d), row(dff), row(dff)] + [_const_spec(a.shape) for a in (gffn, wup, wc, bc, wdown, gfin)],
        out_specs=[row(d), row(dff)],
        out_shape=[jax.ShapeDtypeStruct((t, d), F32), jax.ShapeDtypeStruct((t, dff), F32)],
        compiler_params=_cparams(("arbitrary",)),
        name="ffn_sample",
    )(x1, p1, p2, gffn, wup, wc, bc, wdown, gfin)


def _pad_lanes(a):
    return jnp.pad(a, ((0, 0), (0, LANES - a.shape[1])))


def kernel(x_prompt, x_sample, cache_k, cache_v, state_ssm, state_conv, state_ffn_conv, page_table,
           g_mix, w_in, sb_bias, g_sb_out, w_conv, b_conv, dt_bias, a_log, d_skip, g_ssm, w_out,
           g_ffn, w_up, w_ffn_conv, b_ffn_conv, w_down, g_final):
    assert g_mix.shape[0] == 1, "single-layer step"
    bp, lp, d = x_prompt.shape
    bd, ld, _ = x_sample.shape
    n_pool, page, _, sb_heads, _ = cache_k.shape
    n_pages = page_table.shape[1]
    assert page == CHUNK
    sbw = sb_heads * HEAD_DIM
    heads = a_log.shape[1]
    ssmw = heads * HEAD_DIM
    convd = w_conv.shape[2]
    dff = w_down.shape[1]

    w = w_in[0]
    wqkv = w[:, :3 * sbw].astype(BF16)
    wz = w[:, 3 * sbw:3 * sbw + ssmw].astype(BF16)
    wxbc = w[:, 3 * sbw + ssmw:3 * sbw + ssmw + convd].astype(BF16)
    wdt = _pad_lanes(w[:, 3 * sbw + ssmw + convd:]).astype(BF16)
    wsb = w_out[0, :sbw].astype(BF16)
    wy = w_out[0, sbw:].astype(BF16)
    wup = w_up[0].astype(BF16)
    wdown = w_down[0].astype(BF16)
    ssd_params = (w_conv[0], b_conv, _pad_lanes(dt_bias), _pad_lanes(a_log),
                  jnp.repeat(d_skip, HEAD_DIM, axis=1), g_ssm)
    w2 = _suffix_matrix()
    bias2 = sb_bias[0] * LOG2E
    npairs = sbw // LANES
    parts = jnp.stack(_split3(bias2)).reshape(3, npairs, 2, 1)
    btile = jnp.zeros((npairs, LANES, 2 * CHUNK), BF16).at[:, :3].set(
        jnp.broadcast_to(parts, (3, npairs, 2, CHUNK)).reshape(3, npairs, 2 * CHUNK).transpose(1, 0, 2))

    wkvt = w[:, sbw:3 * sbw].T.astype(BF16)

    def mix_in(x2d, tm, seq_len=None):
        return _inproj(x2d, g_mix, wqkv, wz, wxbc, wdt, wkvt, sbw=sbw, tm=min(tm, x2d.shape[0]),
                       seq_len=seq_len)

    xp2 = x_prompt.reshape(bp * lp, d)
    q, v_tok, z, xbc, dt, kt_p, vt_p = mix_in(xp2, 256, seq_len=lp)
    to_seq = lambda a: a.reshape(bp, lp, a.shape[-1])
    o_sb = _sb_prompt(to_seq(q), kt_p, to_seq(v_tok), btile, w2)
    y, conv_p, ssm_p = _ssd(xbc, z, to_seq(dt), ssd_params, heads=heads, cl=CHUNK, batch=bp, seq_len=lp)
    y_prompt, ffn_p = _ffn_prompt(o_sb, y, x_prompt, (g_sb_out, wsb, wy), g_ffn, wup, w_ffn_conv[0], b_ffn_conv,
                                  wdown, g_final[None], tm=256)

    ts = bd * ld
    xs2 = x_sample.reshape(ts, d)
    q, v_s, z, xbc, dt, k_s = mix_in(xs2, 256)
    to_seq = lambda a: a.reshape(bd, ld, a.shape[-1])
    bias_rows = jnp.broadcast_to(jnp.tile(bias2, ld)[:, None], (ld * sb_heads, 2 * CHUNK))
    pool_t = lambda c: jnp.transpose(c[:, :, 0], (0, 2, 3, 1)).reshape(n_pool, sbw, page)
    o_rows = _sb_sample(to_seq(q), to_seq(k_s), to_seq(v_s), bias_rows, w2, pool_t(cache_k), pool_t(cache_v),
                        page_table, heads=sb_heads, pages=max(g for g in (2, 4, 8, 16, 32) if n_pages % g == 0))
    cl_s = -(-ld // BF16_ROWS) * BF16_ROWS
    y, conv_s, ssm_s = _ssd(xbc, z, dt, ssd_params, heads=heads, cl=cl_s, batch=bd, seq_len=ld,
                            state=(jnp.pad(state_conv[:, 0], ((0, 0), (SUBLANES - state_conv.shape[2], 0), (0, 0))),
                                   state_ssm.reshape(bd, ssmw, D_STATE)))
    x1 = _outproj(o_rows.reshape(ts, sbw), y, xs2, g_sb_out, wsb, wy, tm=ts)
    prev = state_ffn_conv[:, 0]
    p1 = jnp.pad(prev[:, 1:2], ((0, 0), (0, ld - 1), (0, 0))).reshape(ts, dff)
    p2 = jnp.pad(prev, ((0, 0), (0, ld - 2), (0, 0))).reshape(ts, dff)
    y_sample, gate = _ffn_sample(x1, p1, p2, g_ffn, wup, w_ffn_conv[0], b_ffn_conv, wdown, g_final[None],
                                 seq=ld, tm=ts)
    ffn_s = gate.reshape(bd, ld, dff)[:, ld - 2:]

    kv_p = lambda a: jnp.transpose(a.reshape(bp, 1, sb_heads, HEAD_DIM, lp), (0, 4, 1, 2, 3))
    kv_s = lambda a: a.reshape(bd, ld, 1, sb_heads, HEAD_DIM)
    return (y_prompt, y_sample.reshape(bd, ld, d),
            kv_p(kt_p), kv_p(vt_p), ssm_p.reshape(bp, 1, heads, HEAD_DIM, D_STATE),
            conv_p[:, None], ffn_p[:, None],
            kv_s(k_s), kv_s(v_s), ssm_s.reshape(bd, 1, heads, HEAD_DIM, D_STATE),
            conv_s[:, None], ffn_s[:, None])
```

```python
import functools

import jax
import jax.numpy as jnp
from jax import lax
from jax.experimental import pallas as pl
from jax.experimental.pallas import tpu as pltpu

F32 = jnp.float32
BF16 = jnp.bfloat16

HEAD_DIM = 64
D_STATE = 128
SSM_GROUPS = 4
EPS = 1e-6
LOG2E = 1.4426950408889634
LANES = 128
SUBLANES = 8
BF16_ROWS = 16
CHUNK = 128
STEP_BLOCKS = 8
DECODE_ROWS = 16
VMEM_LIMIT = 56 * 1024 * 1024

def _cparams(sem):
    return pltpu.CompilerParams(dimension_semantics=sem, vmem_limit_bytes=VMEM_LIMIT)


def _const_spec(shape):
    nd = len(shape)
    return pl.BlockSpec(shape, lambda *_: (0,) * nd, pipeline_mode=pl.Buffered(1))


def _rms(x, g):
    ms = jnp.mean(x * x, axis=-1, keepdims=True)
    return x * lax.rsqrt(ms + EPS) * g


def _silu(x):
    return x / (1.0 + jnp.exp(-x))


def _dot(a, b):
    return jnp.dot(a, b, preferred_element_type=F32)


def _dot_nt(a, b):
    return lax.dot_general(a, b, (((1,), (1,)), ((), ())), preferred_element_type=F32)


def _split3(x):
    hi = x.astype(BF16)
    r1 = x - hi.astype(F32)
    mid = r1.astype(BF16)
    lo = (r1 - mid.astype(F32)).astype(BF16)
    return hi, mid, lo


def _inproj_body(x_ref, g_ref, wqkv_ref, wz_ref, wxbc_ref, wdt_ref, *rest, sbw, scale, transposed_kv):
    h = _rms(x_ref[...], g_ref[...]).astype(BF16)
    qkv = _dot(h, wqkv_ref[...])
    if transposed_kv:
        wkvt_ref, q_ref, v_ref, z_ref, xbc_ref, dt_ref, kt_ref, vt_ref = rest
        kvt = _dot_nt(wkvt_ref[...], h)
        kt_ref[0] = kvt[:sbw]
        vt_ref[0] = kvt[sbw:]
    else:
        q_ref, v_ref, z_ref, xbc_ref, dt_ref, k_ref = rest
        k_ref[...] = qkv[:, sbw:2 * sbw]
    q_ref[...] = qkv[:, :sbw] * scale
    v_ref[...] = qkv[:, 2 * sbw:]
    for ref, w_ref in ((z_ref, wz_ref), (xbc_ref, wxbc_ref)):
        res = _dot(h, w_ref[...])
        if transposed_kv:
            for j in range(ref.shape[0]):
                ref[j] = res[:, j * LANES:(j + 1) * LANES]
        else:
            ref[...] = res
    dt_ref[...] = _dot(h, wdt_ref[...])


def _inproj(x2d, g, wqkv, wz, wxbc, wdt, wkvt, *, sbw, tm, seq_len=None):
    t, d = x2d.shape
    ssmw, convd = wz.shape[1], wxbc.shape[1]
    row = lambda w: pl.BlockSpec((tm, w), lambda i: (i, 0))
    in_specs = [row(d), _const_spec((1, d)), _const_spec(wqkv.shape), _const_spec(wz.shape),
                _const_spec(wxbc.shape), _const_spec(wdt.shape)]
    args = [x2d, g, wqkv, wz, wxbc, wdt]
    out_specs = [row(sbw), row(sbw), row(ssmw), row(convd), row(LANES)]
    out_shape = [jax.ShapeDtypeStruct((t, w), F32) for w in (sbw, sbw, ssmw, convd, LANES)]
    if seq_len is not None:
        for n, w in ((2, ssmw), (3, convd)):
            out_specs[n] = pl.BlockSpec((w // LANES, tm, LANES), lambda i: (0, i, 0))
            out_shape[n] = jax.ShapeDtypeStruct((w // LANES, t, LANES), F32)
    if seq_len is None:
        out_specs.append(row(sbw))
        out_shape.append(jax.ShapeDtypeStruct((t, sbw), F32))
    else:
        nb = seq_len // tm
        in_specs.append(_const_spec(wkvt.shape))
        args.append(wkvt)
        tr = pl.BlockSpec((1, sbw, tm), lambda i: (i // nb, 0, i % nb))
        out_specs += [tr, tr]
        out_shape += [jax.ShapeDtypeStruct((t // seq_len, sbw, seq_len), F32)] * 2
    return pl.pallas_call(
        functools.partial(_inproj_body, sbw=sbw, scale=HEAD_DIM ** -0.5 * LOG2E, transposed_kv=seq_len is not None),
        grid=(t // tm,),
        in_specs=in_specs,
        out_specs=out_specs,
        out_shape=out_shape,
        compiler_params=_cparams(("arbitrary",)),
        name="inproj",
    )(*args)


def _suffix_matrix():
    r = jnp.arange(2 * CHUNK)[:, None] % CHUNK
    c = jnp.arange(2 * CHUNK)[None, :]
    return jnp.where(c < CHUNK, r >= c, True).astype(BF16)


def _log2_one_minus_sigmoid(z2):
    return jnp.log(1.0 + jnp.exp2(-jnp.abs(z2))) * (-LOG2E) - jnp.maximum(z2, 0.0)


def _split2(x):
    hi = x.astype(BF16)
    return hi, (x - hi.astype(F32)).astype(BF16)


def _sb_prompt_body(q_ref, kt_ref, v_ref, btile_ref, w2_ref, o_ref, kbd_ref, vbd_ref, carry_ref, acc_ref):
    i = pl.program_id(1)
    tq = CHUNK
    npairs, nblk = kbd_ref.shape[0], kbd_ref.shape[1]
    low = lax.broadcasted_iota(jnp.int32, (tq, LANES), 1) < HEAD_DIM

    @pl.when(i == 0)
    def _():
        zero_k = jnp.zeros((HEAD_DIM, tq), BF16)
        for pr in range(npairs):
            rows = slice(pr * LANES, (pr + 1) * LANES)
            for j in range(nblk):
                keys = slice(j * tq, (j + 1) * tq)
                k2 = kt_ref[0, rows, keys].astype(BF16)
                kbd_ref[pr, j, 0:HEAD_DIM, :] = jnp.concatenate([k2[:HEAD_DIM], zero_k], axis=1)
                kbd_ref[pr, j, HEAD_DIM:LANES, :] = jnp.concatenate([zero_k, k2[HEAD_DIM:]], axis=1)
                kbd_ref[pr, j, LANES:, :] = btile_ref[pr]
                v2 = v_ref[0, keys, rows]
                vbd_ref[pr, j, 0:tq, :] = jnp.where(low, v2, 0.0).astype(BF16)
                vbd_ref[pr, j, tq:, :] = jnp.where(low, 0.0, v2).astype(BF16)

    lane = lax.broadcasted_iota(jnp.int32, (tq, LANES), 1)
    bias_cols = jnp.where(lane < 3, 1.0, 0.0)
    qx = [jnp.concatenate([q_ref[0, :, pr * LANES:(pr + 1) * LANES], bias_cols], axis=1).astype(BF16)
          for pr in range(npairs)]
    causal = (lax.broadcasted_iota(jnp.int32, (tq, tq), 1) < lax.broadcasted_iota(jnp.int32, (tq, tq), 0))
    causal2 = jnp.concatenate([causal, causal], axis=1)
    w2 = w2_ref[...]

    def tile(jbs, first):
        zs, parts = [], []
        for pr in range(npairs):
            for nb, jb in enumerate(jbs):
                zz = _dot(qx[pr], kbd_ref[pr, jb])
                l = _log2_one_minus_sigmoid(zz)
                if first and nb == 0:
                    l = jnp.where(causal2, l, 0.0)
                hi, lo = _split2(l)
                for h in range(2):
                    cols = slice(h * tq, (h + 1) * tq)
                    zs.append(zz[:, cols])
                    parts.append(jnp.concatenate([hi[:, cols], lo[:, cols]], axis=1))
        r = _dot(jnp.concatenate(parts, axis=0), w2)
        per_pair = 2 * len(jbs)
        for pr in range(npairs):
            ws = []
            cs = [None, None] if first else [carry_ref[2 * pr], carry_ref[2 * pr + 1]]
            for n in range(pr * per_pair, (pr + 1) * per_pair):
                h = n % 2
                rr = r[n * tq:(n + 1) * tq]
                suf, tot = rr[:, :tq], rr[:, tq:]
                if cs[h] is None:
                    ws.append(jnp.where(causal, jnp.exp2(zs[n] + suf), 0.0))
                    cs[h] = tot
                else:
                    ws.append(jnp.exp2(zs[n] + suf + cs[h]))
                    cs[h] = cs[h] + tot
            carry_ref[2 * pr] = cs[0]
            carry_ref[2 * pr + 1] = cs[1]
            lhs = jnp.concatenate(ws, axis=1).astype(BF16)
            rhs = jnp.concatenate([vbd_ref[pr, jb] for jb in jbs], axis=0)
            if first:
                acc_ref[pr] = _dot(lhs, rhs)
            else:
                acc_ref[pr] += _dot(lhs, rhs)

    for n_first in range(1, STEP_BLOCKS + 1):
        @pl.when(i % STEP_BLOCKS == n_first - 1)
        def _(n_first=n_first):
            tile([i - n for n in range(n_first)], True)

    def body(j, carry):
        top = i - (i % STEP_BLOCKS + 1) - STEP_BLOCKS * j
        tile([top - n for n in range(STEP_BLOCKS)], False)
        return carry

    lax.fori_loop(0, i // STEP_BLOCKS, body, 0)
    o_ref[0] = jnp.concatenate([acc_ref[pr] for pr in range(npairs)], axis=1)


def _sb_prompt(q, kt, v, btile, w2):
    b, l, sbw = q.shape
    tq = CHUNK
    npairs = sbw // LANES
    qspec = pl.BlockSpec((1, tq, sbw), lambda bi, i: (bi, i, 0))
    return pl.pallas_call(
        _sb_prompt_body,
        grid=(b, l // tq),
        in_specs=[qspec, pl.BlockSpec((1, sbw, l), lambda bi, i: (bi, 0, 0)),
                  pl.BlockSpec((1, l, sbw), lambda bi, i: (bi, 0, 0)),
                  _const_spec(btile.shape), _const_spec(w2.shape)],
        out_specs=qspec,
        out_shape=jax.ShapeDtypeStruct((b, l, sbw), F32),
        scratch_shapes=[pltpu.VMEM((npairs, l // tq, 2 * LANES, 2 * tq), BF16),
                        pltpu.VMEM((npairs, l // tq, 2 * tq, LANES), BF16),
                        pltpu.VMEM((2 * npairs, tq, LANES), F32),
                        pltpu.VMEM((npairs, tq, LANES), F32)],
        compiler_params=_cparams(("arbitrary", "arbitrary")),
        name="sb_prompt",
    )(q, kt, v, btile, w2)


def _sb_sample_body(pt_ref, q_ref, kn_ref, vn_ref, bias_ref, w2_ref, *rest, pages, heads, steps):
    kp = rest[:pages]
    vp = rest[pages:2 * pages]
    o_ref, carry_ref, acc_ref = rest[2 * pages:]
    p = pl.program_id(1)
    nt, width = q_ref.shape[1], q_ref.shape[2]
    rows = nt * heads
    q = q_ref[0]
    qrep = jnp.concatenate([jnp.broadcast_to(q[t:t + 1], (heads, width)) for t in range(nt)], axis=0)
    own = (lax.broadcasted_iota(jnp.int32, (rows, width), 1) // HEAD_DIM
           == lax.broadcasted_iota(jnp.int32, (rows, width), 0) % heads)
    qbd = jnp.where(own, qrep, 0.0)
    bias = bias_ref[...]

    @pl.when(p == 0)
    def _():
        kn = kn_ref[0]
        vn = vn_ref[0]
        t_row = lax.broadcasted_iota(jnp.int32, (rows, 1), 0) // heads
        zs, ls = [], []
        for s in range(nt):
            z2 = jnp.sum(qbd * kn[s:s + 1], axis=1, keepdims=True) + bias[:, :1]
            zs.append(z2)
            ls.append(jnp.where(s < t_row, _log2_one_minus_sigmoid(z2), 0.0))
        o = jnp.zeros((rows, width), F32)
        suffix = jnp.zeros((rows, 1), F32)
        for s in reversed(range(nt)):
            w = jnp.where(s < t_row, jnp.exp2(zs[s] + ls[s] + suffix), 0.0)
            o = o + w * vn[s:s + 1]
            suffix = suffix + ls[s]
        carry_ref[...] = jnp.broadcast_to(suffix, (rows, CHUNK))
        acc_ref[...] = o

    w2 = w2_ref[...]
    qb = qbd.astype(BF16)
    carry = carry_ref[...]
    acc = acc_ref[...]
    zs, parts = [], []
    for g in range(0, pages, 2):
        kk = jnp.concatenate([kp[g][0], kp[g + 1][0]], axis=1).astype(BF16)
        zz = _dot(qb, kk) + bias
        hi, lo = _split2(_log2_one_minus_sigmoid(zz))
        for h in range(2):
            cols = slice(h * CHUNK, (h + 1) * CHUNK)
            zs.append(zz[:, cols])
            parts.append(jnp.concatenate([hi[:, cols], lo[:, cols]], axis=1))
    r = _dot(jnp.concatenate(parts, axis=0), w2)
    ws = []
    for g in range(pages):
        rr = r[g * rows:(g + 1) * rows]
        ws.append(jnp.exp2(zs[g] + rr[:, :CHUNK] + carry))
        carry = carry + rr[:, CHUNK:]
    for g in range(0, pages, 2):
        vv = jnp.concatenate([vp[g][0], vp[g + 1][0]], axis=1).astype(BF16)
        acc = acc + _dot_nt(jnp.concatenate(ws[g:g + 2], axis=1).astype(BF16), vv)
    carry_ref[...] = carry
    acc_ref[...] = acc

    @pl.when(p == steps - 1)
    def _():
        mine = jnp.where(own, acc, 0.0)
        o_ref[0] = jnp.concatenate(
            [jnp.sum(mine[t * heads:(t + 1) * heads], axis=0, keepdims=True) for t in range(nt)], axis=0)


def _sb_sample(q, kn, vn, bias_rows, w2, cache_kt, cache_vt, page_table, *, heads, pages):
    bd, nt, width = q.shape
    rows = nt * heads
    n_pages = page_table.shape[1]
    steps = n_pages // pages
    page = cache_kt.shape[2]

    def page_spec(g):
        def imap(b, p, pt):
            return (pt[b, n_pages - 1 - (p * pages + g)], 0, 0)
        return pl.BlockSpec((1, width, page), imap)

    seq = pl.BlockSpec((1, nt, width), lambda b, p, pt: (b, 0, 0))
    grid_spec = pltpu.PrefetchScalarGridSpec(
        num_scalar_prefetch=1,
        grid=(bd, steps),
        in_specs=[seq, seq, seq, _const_spec(bias_rows.shape), _const_spec(w2.shape)]
        + [page_spec(g) for g in range(pages)] * 2,
        out_specs=seq,
        scratch_shapes=[pltpu.VMEM((rows, CHUNK), F32), pltpu.VMEM((rows, width), F32)],
    )
    return pl.pallas_call(
        functools.partial(_sb_sample_body, pages=pages, heads=heads, steps=steps),
        grid_spec=grid_spec,
        out_shape=jax.ShapeDtypeStruct((bd, nt, width), F32),
        compiler_params=_cparams(("arbitrary", "arbitrary")),
        name="sb_sample",
    )(page_table, q, kn, vn, bias_rows, w2, *([cache_kt] * pages), *([cache_vt] * pages))


def _ssd_body(*refs, cl, nv, has_state, heads, ssmw, nseq=1, row0=0):
    if nseq > 1:
        for sq in range(nseq):
            one = lambda r: r.at[pl.ds(sq, 1)]
            views = (list(refs[:3]) + [one(r) for r in refs[3:5]] + list(refs[5:12])
                     + [one(r) for r in refs[12:14]] + [r.at[sq] for r in refs[14:]])
            _ssd_body(*views, cl=cl, nv=nv, has_state=has_state, heads=heads, ssmw=ssmw, row0=sq * nv)
        return
    if has_state:
        (xbc_ref, z_ref, dt_ref, cprev_ref, h0_ref, wconv_ref, bconv_ref, dtb_ref, alog_ref,
         dskip_ref, gssm_ref, y_ref, conv_ref, ssm_ref, xstage, zstage, dtstage) = refs
    else:
        (xbc_ref, z_ref, dt_ref, wconv_ref, bconv_ref, dtb_ref, alog_ref,
         dskip_ref, gssm_ref, y_ref, conv_ref, ssm_ref, ht_ref, xprev) = refs
    c = pl.program_id(1)
    last = pl.num_programs(1) - 1
    taps, convd = wconv_ref.shape
    ncol_s = ssmw // LANES
    hpg = heads // SSM_GROUPS
    ntile = cl // SUBLANES
    assert nv >= taps - 1 and D_STATE == LANES and 2 * HEAD_DIM == LANES
    time_of = lambda p: ntile * (p % SUBLANES) + p // SUBLANES
    row_of = lambda t: SUBLANES * (t % ntile) + t // ntile
    sub = lax.broadcasted_iota(jnp.int32, (SUBLANES, LANES), 0)
    reach = sorted({(a - k) % ntile for a in range(ntile) for k in range(1, taps) if a - k < 0})

    def staged(ref, stage):
        stage[...] = jnp.zeros_like(stage)
        for j in range(stage.shape[0]):
            stage[j, 0:nv, :] = ref[row0:row0 + nv, j * LANES:(j + 1) * LANES]
        return stage

    def tiles(src, j):
        return [src[j, pl.ds(a, SUBLANES, stride=ntile), :] for a in range(ntile)]

    if has_state:
        xsrc, zsrc, dsrc = staged(xbc_ref, xstage), staged(z_ref, zstage), staged(dt_ref, dtstage)
    else:
        xsrc, zsrc, dsrc = xbc_ref, z_ref, dt_ref

        @pl.when(c == 0)
        def _():
            xprev[...] = jnp.zeros_like(xprev)
            ht_ref[...] = jnp.zeros_like(ht_ref)

    def conv_col(j):
        lanes = slice(j * LANES, (j + 1) * LANES)
        cur = tiles(xsrc, j)
        if has_state:
            prev = {}
            for a2 in reach:
                tile = jnp.zeros((SUBLANES, LANES), F32)
                for s in range(SUBLANES):
                    t = ntile * (s - SUBLANES) + a2
                    if t >= -(taps - 1):
                        row = cprev_ref[0, SUBLANES + t:SUBLANES + t + 1, lanes]
                        tile = jnp.where(sub == s, jnp.broadcast_to(row, (SUBLANES, LANES)), tile)
                prev[a2] = tile
        else:
            prev = {a2: xprev[n, :, lanes] for n, a2 in enumerate(reach)}
        w = [wconv_ref[taps - 1 - k:taps - k, lanes] for k in range(taps)]
        pieces = []
        for a in range(ntile):
            acc = bconv_ref[:, lanes] + cur[a] * w[0]
            for k in range(1, taps):
                a2, r = (a - k) % ntile, -((a - k) // ntile)
                if r == 0:
                    sh = cur[a2]
                else:
                    sh = jnp.where(sub < r, pltpu.roll(prev[a2], r, axis=0), pltpu.roll(cur[a2], r, axis=0))
                acc = acc + sh * w[k]
            pieces.append(acc)
        conv_ref[0, :, lanes] = jnp.concatenate(
            [cur[t % ntile][t // ntile:t // ntile + 1] for t in range(nv - (taps - 1), nv)], axis=0)
        if not has_state:
            for n, a2 in enumerate(reach):
                xprev[n, :, lanes] = cur[a2]
        return _silu(jnp.concatenate(pieces, axis=0))

    x = jnp.concatenate(tiles(dsrc, 0), axis=0) + dtb_ref[...]
    dt = jnp.maximum(x, 0.0) + jnp.log1p(jnp.exp(-jnp.abs(x)))
    if nv < cl:
        dt = jnp.where(time_of(lax.broadcasted_iota(jnp.int32, (cl, LANES), 0)) < nv, dt, 0.0)
    a = -jnp.exp(alog_ref[...])
    da = dt * a
    causal = (time_of(lax.broadcasted_iota(jnp.int32, (cl, cl), 1))
              <= time_of(lax.broadcasted_iota(jnp.int32, (cl, cl), 0)))
    ltri = jnp.where(causal, 1.0, 0.0).astype(BF16)
    acum = sum(_dot(ltri, part) for part in _split3(da))
    acum = acum * LOG2E
    acum_last = acum[cl - 1:cl, :]
    eac = jnp.exp2(acum)
    wgt = dt * jnp.exp2(acum_last - acum)
    cdec = jnp.exp2(acum_last)
    acum_t = acum.T
    src_t = acum_t - jnp.log2(dt.T)

    low = lax.broadcasted_iota(jnp.int32, (cl, LANES), 1) < HEAD_DIM
    low1 = low[:1]
    low_rows = lax.broadcasted_iota(jnp.int32, (LANES, D_STATE), 0) < HEAD_DIM

    ycols = []
    for g in range(SSM_GROUPS):
        bc = conv_col(ncol_s + g)
        cc = conv_col(ncol_s + SSM_GROUPS + g)
        bc_b = bc.astype(BF16)
        cb = _dot_nt(cc.astype(BF16), bc_b)
        if not has_state:
            bc_tb = bc.T.astype(BF16)
        for jp in range(hpg // 2):
            h0 = g * hpg + 2 * jp
            col = h0 // 2
            lanes = slice(col * LANES, (col + 1) * LANES)
            xs_p = conv_col(col)
            xs_pb = xs_p.astype(BF16)
            st = h0_ref[0, lanes, :] if has_state else ht_ref[:, lanes]
            st_b = st.astype(BF16)
            yh = []
            for hh in (h0, h0 + 1):
                seg = acum[:, hh:hh + 1] - src_t[hh:hh + 1, :]
                m = (cb * jnp.where(causal, jnp.exp2(seg), 0.0)).astype(BF16)
                off = (cc * eac[:, hh:hh + 1]).astype(BF16)
                if has_state:
                    yh.append(_dot(m, xs_pb) + _dot_nt(off, st_b))
                else:
                    yh.append(_dot(jnp.concatenate([m, off], axis=1), jnp.concatenate([xs_pb, st_b], axis=0)))
            xs_w = (xs_p * jnp.where(low, wgt[:, h0:h0 + 1], wgt[:, h0 + 1:h0 + 2])).astype(BF16)
            if has_state:
                s_new = lax.dot_general(xs_w, bc_b, (((0,), (0,)), ((), ())), preferred_element_type=F32)
                ssm_ref[0, lanes, :] = jnp.where(low_rows, cdec[:, h0:h0 + 1], cdec[:, h0 + 1:h0 + 2]) * st + s_new
            else:
                cd = jnp.where(low1, cdec[:, h0:h0 + 1], cdec[:, h0 + 1:h0 + 2])
                ht_ref[:, lanes] = cd * st + _dot(bc_tb, xs_w)
            zin = jnp.concatenate(tiles(zsrc, col), axis=0)
            ycols.append((jnp.where(low, yh[0], yh[1]) + dskip_ref[:, lanes] * xs_p) * _silu(zin))

    per_group = ncol_s // SSM_GROUPS
    for g in range(SSM_GROUPS):
        group = ycols[g * per_group:(g + 1) * per_group]
        ms = sum(jnp.sum(yc * yc, axis=-1, keepdims=True) for yc in group) * (1.0 / (per_group * LANES))
        scale = lax.rsqrt(ms + EPS)
        for n, yc in enumerate(group):
            col = g * per_group + n
            lanes = slice(col * LANES, (col + 1) * LANES)
            out = yc * scale * gssm_ref[:, lanes]
            if nv < cl:
                y_ref[row0:row0 + nv, lanes] = jnp.concatenate(
                    [out[row_of(t):row_of(t) + 1] for t in range(nv)], axis=0)
            else:
                for a in range(ntile):
                    y_ref[col, pl.ds(a, SUBLANES, stride=ntile), :] = out[a * SUBLANES:(a + 1) * SUBLANES]

    if not has_state:
        @pl.when(c == last)
        def _():
            ssm_ref[0] = ht_ref[...].T


def _ssd(xbc, z, dt, params, *, heads, cl, batch, seq_len, state=None):
    has_state = state is not None
    l = seq_len
    taps, convd = params[0].shape
    ssmw = heads * HEAD_DIM
    nseq = DECODE_ROWS // l if has_state else 1
    per_seq = lambda r, w: pl.BlockSpec((nseq, r, w), lambda bi, c: (bi, 0, 0))
    const = lambda a: _const_spec(a.shape)
    scratch = []
    if has_state:
        assert nseq * l == DECODE_ROWS and nseq > 1 and batch % nseq == 0
        nv, nchunks = l, 1
        blk = lambda w: pl.BlockSpec((nseq * nv, w), lambda bi, c: (bi, 0))
        y_shape = (batch * l, ssmw)
        in_specs = [blk(convd), blk(ssmw), blk(LANES), per_seq(SUBLANES, convd), per_seq(ssmw, D_STATE)]
        args = [xbc, z, dt] + list(state)
        scratch += [pltpu.VMEM((nseq, w // LANES, cl, LANES), F32) for w in (convd, ssmw, LANES)]
    else:
        nv, nchunks = cl, l // cl
        blk = lambda w: pl.BlockSpec((w // LANES, cl, LANES), lambda bi, c: (0, bi * nchunks + c, 0))
        y_shape = (ssmw // LANES, batch * l, LANES)
        in_specs = [blk(convd), blk(ssmw), pl.BlockSpec((1, cl, LANES), lambda bi, c: (bi, c, 0))]
        args = [xbc, z, dt]
        scratch += [pltpu.VMEM((D_STATE, ssmw), F32),
                    pltpu.VMEM((min(taps - 1, cl // SUBLANES), SUBLANES, convd), F32)]
    in_specs += [const(p) for p in params]
    args += list(params)
    return pl.pallas_call(
        functools.partial(_ssd_body, cl=cl, nv=nv, has_state=has_state, heads=heads, ssmw=ssmw, nseq=nseq),
        grid=(batch // nseq, nchunks),
        in_specs=in_specs,
        out_specs=[blk(ssmw), per_seq(taps - 1, convd), per_seq(ssmw, D_STATE)],
        out_shape=[jax.ShapeDtypeStruct(y_shape, F32),
                   jax.ShapeDtypeStruct((batch, taps - 1, convd), F32),
                   jax.ShapeDtypeStruct((batch, ssmw, D_STATE), F32)],
        scratch_shapes=scratch,
        compiler_params=_cparams(("arbitrary", "arbitrary")),
        name="ssd_sample" if has_state else "ssd_prompt",
    )(*args)


def _mix_out(o_sb, y_ref, x, gsb_ref, wsb_ref, wy_ref):
    o = _rms(o_sb, gsb_ref[...]).astype(BF16)
    if len(y_ref.shape) == 3:
        y = jnp.concatenate([y_ref[j] for j in range(y_ref.shape[0])], axis=1)
    else:
        y = y_ref[...]
    return x + _dot(o, wsb_ref[...]) + _dot(y.astype(BF16), wy_ref[...])


def _outproj_body(o_ref, y_ref, x_ref, gsb_ref, wsb_ref, wy_ref, x1_ref):
    x1_ref[...] = _mix_out(o_ref[...], y_ref, x_ref[...], gsb_ref, wsb_ref, wy_ref)


def _outproj(o_sb, y, x2d, gsb, wsb, wy, *, tm):
    t, d = x2d.shape
    row = lambda w: pl.BlockSpec((tm, w), lambda i: (i, 0))
    y_spec = row(y.shape[1]) if y.ndim == 2 else pl.BlockSpec((y.shape[0], tm, LANES), lambda i: (0, i, 0))
    return pl.pallas_call(
        _outproj_body,
        grid=(t // tm,),
        in_specs=[row(o_sb.shape[1]), y_spec, row(d), _const_spec(gsb.shape),
                  _const_spec(wsb.shape), _const_spec(wy.shape)],
        out_specs=row(d),
        out_shape=jax.ShapeDtypeStruct((t, d), F32),
        compiler_params=_cparams(("arbitrary",)),
        name="outproj",
    )(o_sb, y, x2d, gsb, wsb, wy)


def _ffn_tail(x1, gate, g1, g2, val, wc_ref, bc_ref, wdown_ref, gfin_ref):
    gc = bc_ref[...] + wc_ref[2:3, :] * gate + wc_ref[1:2, :] * g1 + wc_ref[0:1, :] * g2
    u = (_silu(gc) * val).astype(BF16)
    return _rms(x1 + _dot(u, wdown_ref[...]), gfin_ref[...])


def _ffn_prompt_body(o_ref, ymix_ref, x_ref, gsb_ref, wsb_ref, wy_ref, gffn_ref, wup_ref, wc_ref, bc_ref,
                     wdown_ref, gfin_ref, y_ref, fnew_ref, gbuf, *, tm, dff):
    j = pl.program_id(1)

    @pl.when(j == 0)
    def _():
        gbuf[0:SUBLANES, :] = jnp.zeros((SUBLANES, dff), F32)

    x1 = _mix_out(o_ref[0], ymix_ref, x_ref[0], gsb_ref, wsb_ref, wy_ref)
    up =_dot(_rms(x1, gffn_ref[...]).astype(BF16), wup_ref[...])
    gate = up[:, :dff]
    gbuf[SUBLANES:SUBLANES + tm, :] = gate
    g1 = gbuf[SUBLANES - 1:SUBLANES - 1 + tm, :]
    g2 = gbuf[SUBLANES - 2:SUBLANES - 2 + tm, :]
    y_ref[0] = _ffn_tail(x1, gate, g1, g2, up[:, dff:], wc_ref, bc_ref, wdown_ref, gfin_ref)
    gbuf[0:SUBLANES, :] = gbuf[tm:tm + SUBLANES, :]

    @pl.when(j == pl.num_programs(1) - 1)
    def _():
        fnew_ref[0] = gbuf[SUBLANES - 2:SUBLANES, :]


def _ffn_prompt(o_sb, y_t, x, mix_w, gffn, wup, wc, bc, wdown, gfin, *, tm):
    b, l, d = x.shape
    dff = wdown.shape[0]
    nb = l // tm
    blk = pl.BlockSpec((1, tm, d), lambda bi, j: (bi, j, 0))
    const = lambda a: _const_spec(a.shape)
    consts = tuple(mix_w) + (gffn, wup, wc, bc, wdown, gfin)
    return pl.pallas_call(
        functools.partial(_ffn_prompt_body, tm=tm, dff=dff),
        grid=(b, nb),
        in_specs=[pl.BlockSpec((1, tm, o_sb.shape[2]), lambda bi, j: (bi, j, 0)),
                  pl.BlockSpec((y_t.shape[0], tm, LANES), lambda bi, j: (0, bi * nb + j, 0)),
                  blk] + [const(a) for a in consts],
        out_specs=[blk, pl.BlockSpec((1, 2, dff), lambda bi, j: (bi, 0, 0))],
        out_shape=[jax.ShapeDtypeStruct((b, l, d), F32), jax.ShapeDtypeStruct((b, 2, dff), F32)],
        scratch_shapes=[pltpu.VMEM((tm + SUBLANES, dff), F32)],
        compiler_params=_cparams(("arbitrary", "arbitrary")),
        name="ffn_prompt",
    )(o_sb, y_t, x, *consts)


def _ffn_sample_body(x1_ref, p1_ref, p2_ref, gffn_ref, wup_ref, wc_ref, bc_ref, wdown_ref, gfin_ref,
                     y_ref, gate_ref, *, seq, dff):
    x1 = x1_ref[...]
    up = _dot(_rms(x1, gffn_ref[...]).astype(BF16), wup_ref[...])
    gate = up[:, :dff]
    gate_ref[...] = gate
    pos = lax.broadcasted_iota(jnp.int32, (x1.shape[0], 1), 0) % seq
    g1 = jnp.where(pos >= 1, pltpu.roll(gate, 1, axis=0), p1_ref[...])
    g2 = jnp.where(pos >= 2, pltpu.roll(gate, 2, axis=0), p2_ref[...])
    y_ref[...] = _ffn_tail(x1, gate, g1, g2, up[:, dff:], wc_ref, bc_ref, wdown_ref, gfin_ref)


def _ffn_sample(x1, p1, p2, gffn, wup, wc, bc, wdown, gfin, *, seq, tm):
    t, d = x1.shape
    dff = wdown.shape[0]
    row = lambda w: pl.BlockSpec((tm, w), lambda i: (i, 0))
    return pl.pallas_call(
        functools.partial(_ffn_sample_body, seq=seq, dff=dff),
        grid=(t // tm,),
        in_specs=[row(d), row(dff), row(dff)] + [_const_spec(a.shape) for a in (gffn, wup, wc, bc, wdown, gfin)],
        out_specs=[row(d), row(dff)],
        out_shape=[jax.ShapeDtypeStruct((t, d), F32), jax.ShapeDtypeStruct((t, dff), F32)],
        compiler_params=_cparams(("arbitrary",)),
        name="ffn_sample",
    )(x1, p1, p2, gffn, wup, wc, bc, wdown, gfin)


def _pad_lanes(a):
    return jnp.pad(a, ((0, 0), (0, LANES - a.shape[1])))


def kernel(x_prompt, x_sample, cache_k, cache_v, state_ssm, state_conv, state_ffn_conv, page_table,
           g_mix, w_in, sb_bias, g_sb_out, w_conv, b_conv, dt_bias, a_log, d_skip, g_ssm, w_out,
           g_ffn, w_up, w_ffn_conv, b_ffn_conv, w_down, g_final):
    assert g_mix.shape[0] == 1, "single-layer step"
    bp, lp, d = x_prompt.shape
    bd, ld, _ = x_sample.shape
    n_pool, page, _, sb_heads, _ = cache_k.shape
    n_pages = page_table.shape[1]
    assert page == CHUNK
    sbw = sb_heads * HEAD_DIM
    heads = a_log.shape[1]
    ssmw = heads * HEAD_DIM
    convd = w_conv.shape[2]
    dff = w_down.shape[1]

    w = w_in[0]
    wqkv = w[:, :3 * sbw].astype(BF16)
    wz = w[:, 3 * sbw:3 * sbw + ssmw].astype(BF16)
    wxbc = w[:, 3 * sbw + ssmw:3 * sbw + ssmw + convd].astype(BF16)
    wdt = _pad_lanes(w[:, 3 * sbw + ssmw + convd:]).astype(BF16)
    wsb = w_out[0, :sbw].astype(BF16)
    wy = w_out[0, sbw:].astype(BF16)
    wup = w_up[0].astype(BF16)
    wdown = w_down[0].astype(BF16)
    ssd_params = (w_conv[0], b_conv, _pad_lanes(dt_bias), _pad_lanes(a_log),
                  jnp.repeat(d_skip, HEAD_DIM, axis=1), g_ssm)
    w2 = _suffix_matrix()
    bias2 = sb_bias[0] * LOG2E
    npairs = sbw // LANES
    parts = jnp.stack(_split3(bias2)).reshape(3, npairs, 2, 1)
    btile = jnp.zeros((npairs, LANES, 2 * CHUNK), BF16).at[:, :3].set(
        jnp.broadcast_to(parts, (3, npairs, 2, CHUNK)).reshape(3, npairs, 2 * CHUNK).transpose(1, 0, 2))

    wkvt = w[:, sbw:3 * sbw].T.astype(BF16)

    def mix_in(x2d, tm, seq_len=None):
        return _inproj(x2d, g_mix, wqkv, wz, wxbc, wdt, wkvt, sbw=sbw, tm=min(tm, x2d.shape[0]),
                       seq_len=seq_len)

    xp2 = x_prompt.reshape(bp * lp, d)
    q, v_tok, z, xbc, dt, kt_p, vt_p = mix_in(xp2, 256, seq_len=lp)
    to_seq = lambda a: a.reshape(bp, lp, a.shape[-1])
    o_sb = _sb_prompt(to_seq(q), kt_p, to_seq(v_tok), btile, w2)
    y, conv_p, ssm_p = _ssd(xbc, z, to_seq(dt), ssd_params, heads=heads, cl=CHUNK, batch=bp, seq_len=lp)
    y_prompt, ffn_p = _ffn_prompt(o_sb, y, x_prompt, (g_sb_out, wsb, wy), g_ffn, wup, w_ffn_conv[0], b_ffn_conv,
                                  wdown, g_final[None], tm=256)

    ts = bd * ld
    xs2 = x_sample.reshape(ts, d)
    q, v_s, z, xbc, dt, k_s = mix_in(xs2, 256)
    to_seq = lambda a: a.reshape(bd, ld, a.shape[-1])
    bias_rows = jnp.broadcast_to(jnp.tile(bias2, ld)[:, None], (ld * sb_heads, 2 * CHUNK))
    pool_t = lambda c: jnp.transpose(c[:, :, 0], (0, 2, 3, 1)).reshape(n_pool, sbw, page)
    o_rows = _sb_sample(to_seq(q), to_seq(k_s), to_seq(v_s), bias_rows, w2, pool_t(cache_k), pool_t(cache_v),
                        page_table, heads=sb_heads, pages=max(g for g in (2, 4, 8, 16, 32) if n_pages % g == 0))
    cl_s = -(-ld // BF16_ROWS) * BF16_ROWS
    y, conv_s, ssm_s = _ssd(xbc, z, dt, ssd_params, heads=heads, cl=cl_s, batch=bd, seq_len=ld,
                            state=(jnp.pad(state_conv[:, 0], ((0, 0), (SUBLANES - state_conv.shape[2], 0), (0, 0))),
                                   state_ssm.reshape(bd, ssmw, D_STATE)))
    x1 = _outproj(o_rows.reshape(ts, sbw), y, xs2, g_sb_out, wsb, wy, tm=ts)
    prev = state_ffn_conv[:, 0]
    p1 = jnp.pad(prev[:, 1:2], ((0, 0), (0, ld - 1), (0, 0))).reshape(ts, dff)
    p2 = jnp.pad(prev, ((0, 0), (0, ld - 2), (0, 0))).reshape(ts, dff)
    y_sample, gate = _ffn_sample(x1, p1, p2, g_ffn, wup, w_ffn_conv[0], b_ffn_conv, wdown, g_final[None],
                                 seq=ld, tm=ts)
    ffn_s = gate.reshape(bd, ld, dff)[:, ld - 2:]

    kv_p = lambda a: jnp.transpose(a.reshape(bp, 1, sb_heads, HEAD_DIM, lp), (0, 4, 1, 2, 3))
    kv_s = lambda a: a.reshape(bd, ld, 1, sb_heads, HEAD_DIM)
    return (y_prompt, y_sample.reshape(bd, ld, d),
            kv_p(kt_p), kv_p(vt_p), ssm_p.reshape(bp, 1, heads, HEAD_DIM, D_STATE),
            conv_p[:, None], ffn_p[:, None],
            kv_s(k_s), kv_s(v_s), ssm_s.reshape(bd, 1, heads, HEAD_DIM, D_STATE),
            conv_s[:, None], ffn_s[:, None])
```

```python
import functools

import jax
import jax.numpy as jnp
from jax import lax
from jax.experimental import pallas as pl
from jax.experimental.pallas import tpu as pltpu

F32 = jnp.float32
BF16 = jnp.bfloat16

HEAD_DIM = 64
D_STATE = 128
SSM_GROUPS = 4
EPS = 1e-6
LOG2E = 1.4426950408889634
LANES = 128
SUBLANES = 8
BF16_ROWS = 16
CHUNK = 128
STEP_BLOCKS = 8
DECODE_ROWS = 16
VMEM_LIMIT = 56 * 1024 * 1024

def _cparams(sem):
    return pltpu.CompilerParams(dimension_semantics=sem, vmem_limit_bytes=VMEM_LIMIT)


def _const_spec(shape):
    nd = len(shape)
    return pl.BlockSpec(shape, lambda *_: (0,) * nd, pipeline_mode=pl.Buffered(1))


def _rms(x, g):
    ms = jnp.mean(x * x, axis=-1, keepdims=True)
    return x * lax.rsqrt(ms + EPS) * g


def _silu(x):
    return x / (1.0 + jnp.exp(-x))


def _dot(a, b):
    return jnp.dot(a, b, preferred_element_type=F32)


def _dot_nt(a, b):
    return lax.dot_general(a, b, (((1,), (1,)), ((), ())), preferred_element_type=F32)


def _split3(x):
    hi = x.astype(BF16)
    r1 = x - hi.astype(F32)
    mid = r1.astype(BF16)
    lo = (r1 - mid.astype(F32)).astype(BF16)
    return hi, mid, lo


def _inproj_body(x_ref, g_ref, wqkv_ref, wz_ref, wxbc_ref, wdt_ref, *rest, sbw, scale, transposed_kv):
    h = _rms(x_ref[...], g_ref[...]).astype(BF16)
    qkv = _dot(h, wqkv_ref[...])
    if transposed_kv:
        wkvt_ref, q_ref, v_ref, z_ref, xbc_ref, dt_ref, kt_ref, vt_ref = rest
        kvt = _dot_nt(wkvt_ref[...], h)
        kt_ref[0] = kvt[:sbw]
        vt_ref[0] = kvt[sbw:]
    else:
        q_ref, v_ref, z_ref, xbc_ref, dt_ref, k_ref = rest
        k_ref[...] = qkv[:, sbw:2 * sbw]
    q_ref[...] = qkv[:, :sbw] * scale
    v_ref[...] = qkv[:, 2 * sbw:]
    for ref, w_ref in ((z_ref, wz_ref), (xbc_ref, wxbc_ref)):
        res = _dot(h, w_ref[...])
        if transposed_kv:
            for j in range(ref.shape[0]):
                ref[j] = res[:, j * LANES:(j + 1) * LANES]
        else:
            ref[...] = res
    dt_ref[...] = _dot(h, wdt_ref[...])


def _inproj(x2d, g, wqkv, wz, wxbc, wdt, wkvt, *, sbw, tm, seq_len=None):
    t, d = x2d.shape
    ssmw, convd = wz.shape[1], wxbc.shape[1]
    row = lambda w: pl.BlockSpec((tm, w), lambda i: (i, 0))
    in_specs = [row(d), _const_spec((1, d)), _const_spec(wqkv.shape), _const_spec(wz.shape),
                _const_spec(wxbc.shape), _const_spec(wdt.shape)]
    args = [x2d, g, wqkv, wz, wxbc, wdt]
    out_specs = [row(sbw), row(sbw), row(ssmw), row(convd), row(LANES)]
    out_shape = [jax.ShapeDtypeStruct((t, w), F32) for w in (sbw, sbw, ssmw, convd, LANES)]
    if seq_len is not None:
        for n, w in ((2, ssmw), (3, convd)):
            out_specs[n] = pl.BlockSpec((w // LANES, tm, LANES), lambda i: (0, i, 0))
            out_shape[n] = jax.ShapeDtypeStruct((w // LANES, t, LANES), F32)
    if seq_len is None:
        out_specs.append(row(sbw))
        out_shape.append(jax.ShapeDtypeStruct((t, sbw), F32))
    else:
        nb = seq_len // tm
        in_specs.append(_const_spec(wkvt.shape))
        args.append(wkvt)
        tr = pl.BlockSpec((1, sbw, tm), lambda i: (i // nb, 0, i % nb))
        out_specs += [tr, tr]
        out_shape += [jax.ShapeDtypeStruct((t // seq_len, sbw, seq_len), F32)] * 2
    return pl.pallas_call(
        functools.partial(_inproj_body, sbw=sbw, scale=HEAD_DIM ** -0.5 * LOG2E, transposed_kv=seq_len is not None),
        grid=(t // tm,),
        in_specs=in_specs,
        out_specs=out_specs,
        out_shape=out_shape,
        compiler_params=_cparams(("arbitrary",)),
        name="inproj",
    )(*args)


def _suffix_matrix():
    r = jnp.arange(2 * CHUNK)[:, None] % CHUNK
    c = jnp.arange(2 * CHUNK)[None, :]
    return jnp.where(c < CHUNK, r >= c, True).astype(BF16)


def _log2_one_minus_sigmoid(z2):
    return jnp.log(1.0 + jnp.exp2(-jnp.abs(z2))) * (-LOG2E) - jnp.maximum(z2, 0.0)


def _split2(x):
    hi = x.astype(BF16)
    return hi, (x - hi.astype(F32)).astype(BF16)


def _sb_prompt_body(q_ref, kt_ref, v_ref, btile_ref, w2_ref, o_ref, kbd_ref, vbd_ref, carry_ref, acc_ref):
    i = pl.program_id(1)
    tq = CHUNK
    npairs, nblk = kbd_ref.shape[0], kbd_ref.shape[1]
    low = lax.broadcasted_iota(jnp.int32, (tq, LANES), 1) < HEAD_DIM

    @pl.when(i == 0)
    def _():
        zero_k = jnp.zeros((HEAD_DIM, tq), BF16)
        for pr in range(npairs):
            rows = slice(pr * LANES, (pr + 1) * LANES)
            for j in range(nblk):
                keys = slice(j * tq, (j + 1) * tq)
                k2 = kt_ref[0, rows, keys].astype(BF16)
                kbd_ref[pr, j, 0:HEAD_DIM, :] = jnp.concatenate([k2[:HEAD_DIM], zero_k], axis=1)
                kbd_ref[pr, j, HEAD_DIM:LANES, :] = jnp.concatenate([zero_k, k2[HEAD_DIM:]], axis=1)
                kbd_ref[pr, j, LANES:, :] = btile_ref[pr]
                v2 = v_ref[0, keys, rows]
                vbd_ref[pr, j, 0:tq, :] = jnp.where(low, v2, 0.0).astype(BF16)
                vbd_ref[pr, j, tq:, :] = jnp.where(low, 0.0, v2).astype(BF16)

    lane = lax.broadcasted_iota(jnp.int32, (tq, LANES), 1)
    bias_cols = jnp.where(lane < 3, 1.0, 0.0)
    qx = [jnp.concatenate([q_ref[0, :, pr * LANES:(pr + 1) * LANES], bias_cols], axis=1).astype(BF16)
          for pr in range(npairs)]
    causal = (lax.broadcasted_iota(jnp.int32, (tq, tq), 1) < lax.broadcasted_iota(jnp.int32, (tq, tq), 0))
    causal2 = jnp.concatenate([causal, causal], axis=1)
    w2 = w2_ref[...]

    def tile(jbs, first):
        zs, parts = [], []
        for pr in range(npairs):
            for nb, jb in enumerate(jbs):
                zz = _dot(qx[pr], kbd_ref[pr, jb])
                l = _log2_one_minus_sigmoid(zz)
                if first and nb == 0:
                    l = jnp.where(causal2, l, 0.0)
                hi, lo = _split2(l)
                for h in range(2):
                    cols = slice(h * tq, (h + 1) * tq)
                    zs.append(zz[:, cols])
                    parts.append(jnp.concatenate([hi[:, cols], lo[:, cols]], axis=1))
        r = _dot(jnp.concatenate(parts, axis=0), w2)
        per_pair = 2 * len(jbs)
        for pr in range(npairs):
            ws = []
            cs = [None, None] if first else [carry_ref[2 * pr], carry_ref[2 * pr + 1]]
            for n in range(pr * per_pair, (pr + 1) * per_pair):
                h = n % 2
                rr = r[n * tq:(n + 1) * tq]
                suf, tot = rr[:, :tq], rr[:, tq:]
                if cs[h] is None:
                    ws.append(jnp.where(causal, jnp.exp2(zs[n] + suf), 0.0))
                    cs[h] = tot
                else:
                    ws.append(jnp.exp2(zs[n] + suf + cs[h]))
                    cs[h] = cs[h] + tot
            carry_ref[2 * pr] = cs[0]
            carry_ref[2 * pr + 1] = cs[1]
            lhs = jnp.concatenate(ws, axis=1).astype(BF16)
            rhs = jnp.concatenate([vbd_ref[pr, jb] for jb in jbs], axis=0)
            if first:
                acc_ref[pr] = _dot(lhs, rhs)
            else:
                acc_ref[pr] += _dot(lhs, rhs)

    for n_first in range(1, STEP_BLOCKS + 1):
        @pl.when(i % STEP_BLOCKS == n_first - 1)
        def _(n_first=n_first):
            tile([i - n for n in range(n_first)], True)

    def body(j, carry):
        top = i - (i % STEP_BLOCKS + 1) - STEP_BLOCKS * j
        tile([top - n for n in range(STEP_BLOCKS)], False)
        return carry

    lax.fori_loop(0, i // STEP_BLOCKS, body, 0)
    o_ref[0] = jnp.concatenate([acc_ref[pr] for pr in range(npairs)], axis=1)


def _sb_prompt(q, kt, v, btile, w2):
    b, l, sbw = q.shape
    tq = CHUNK
    npairs = sbw // LANES
    qspec = pl.BlockSpec((1, tq, sbw), lambda bi, i: (bi, i, 0))
    return pl.pallas_call(
        _sb_prompt_body,
        grid=(b, l // tq),
        in_specs=[qspec, pl.BlockSpec((1, sbw, l), lambda bi, i: (bi, 0, 0)),
                  pl.BlockSpec((1, l, sbw), lambda bi, i: (bi, 0, 0)),
                  _const_spec(btile.shape), _const_spec(w2.shape)],
        out_specs=qspec,
        out_shape=jax.ShapeDtypeStruct((b, l, sbw), F32),
        scratch_shapes=[pltpu.VMEM((npairs, l // tq, 2 * LANES, 2 * tq), BF16),
                        pltpu.VMEM((npairs, l // tq, 2 * tq, LANES), BF16),
                        pltpu.VMEM((2 * npairs, tq, LANES), F32),
                        pltpu.VMEM((npairs, tq, LANES), F32)],
        compiler_params=_cparams(("arbitrary", "arbitrary")),
        name="sb_prompt",
    )(q, kt, v, btile, w2)


def _sb_sample_body(pt_ref, q_ref, kn_ref, vn_ref, bias_ref, w2_ref, *rest, pages, heads, steps):
    kp = rest[:pages]
    vp = rest[pages:2 * pages]
    o_ref, carry_ref, acc_ref = rest[2 * pages:]
    p = pl.program_id(1)
    nt, width = q_ref.shape[1], q_ref.shape[2]
    rows = nt * heads
    q = q_ref[0]
    qrep = jnp.concatenate([jnp.broadcast_to(q[t:t + 1], (heads, width)) for t in range(nt)], axis=0)
    own = (lax.broadcasted_iota(jnp.int32, (rows, width), 1) // HEAD_DIM
           == lax.broadcasted_iota(jnp.int32, (rows, width), 0) % heads)
    qbd = jnp.where(own, qrep, 0.0)
    bias = bias_ref[...]

    @pl.when(p == 0)
    def _():
        kn = kn_ref[0]
        vn = vn_ref[0]
        t_row = lax.broadcasted_iota(jnp.int32, (rows, 1), 0) // heads
        zs, ls = [], []
        for s in range(nt):
            z2 = jnp.sum(qbd * kn[s:s + 1], axis=1, keepdims=True) + bias[:, :1]
            zs.append(z2)
            ls.append(jnp.where(s < t_row, _log2_one_minus_sigmoid(z2), 0.0))
        o = jnp.zeros((rows, width), F32)
        suffix = jnp.zeros((rows, 1), F32)
        for s in reversed(range(nt)):
            w = jnp.where(s < t_row, jnp.exp2(zs[s] + ls[s] + suffix), 0.0)
            o = o + w * vn[s:s + 1]
            suffix = suffix + ls[s]
        carry_ref[...] = jnp.broadcast_to(suffix, (rows, CHUNK))
        acc_ref[...] = o

    w2 = w2_ref[...]
    qb = qbd.astype(BF16)
    carry = carry_ref[...]
    acc = acc_ref[...]
    zs, parts = [], []
    for g in range(0, pages, 2):
        kk = jnp.concatenate([kp[g][0], kp[g + 1][0]], axis=1).astype(BF16)
        zz = _dot(qb, kk) + bias
        hi, lo = _split2(_log2_one_minus_sigmoid(zz))
        for h in range(2):
            cols = slice(h * CHUNK, (h + 1) * CHUNK)
            zs.append(zz[:, cols])
            parts.append(jnp.concatenate([hi[:, cols], lo[:, cols]], axis=1))
    r = _dot(jnp.concatenate(parts, axis=0), w2)
    ws = []
    for g in range(pages):
        rr = r[g * rows:(g + 1) * rows]
        ws.append(jnp.exp2(zs[g] + rr[:, :CHUNK] + carry))
        carry = carry + rr[:, CHUNK:]
    for g in range(0, pages, 2):
        vv = jnp.concatenate([vp[g][0], vp[g + 1][0]], axis=1).astype(BF16)
        acc = acc + _dot_nt(jnp.concatenate(ws[g:g + 2], axis=1).astype(BF16), vv)
    carry_ref[...] = carry
    acc_ref[...] = acc

    @pl.when(p == steps - 1)
    def _():
        mine = jnp.where(own, acc, 0.0)
        o_ref[0] = jnp.concatenate(
            [jnp.sum(mine[t * heads:(t + 1) * heads], axis=0, keepdims=True) for t in range(nt)], axis=0)


def _sb_sample(q, kn, vn, bias_rows, w2, cache_kt, cache_vt, page_table, *, heads, pages):
    bd, nt, width = q.shape
    rows = nt * heads
    n_pages = page_table.shape[1]
    steps = n_pages // pages
    page = cache_kt.shape[2]

    def page_spec(g):
        def imap(b, p, pt):
            return (pt[b, n_pages - 1 - (p * pages + g)], 0, 0)
        return pl.BlockSpec((1, width, page), imap)

    seq = pl.BlockSpec((1, nt, width), lambda b, p, pt: (b, 0, 0))
    grid_spec = pltpu.PrefetchScalarGridSpec(
        num_scalar_prefetch=1,
        grid=(bd, steps),
        in_specs=[seq, seq, seq, _const_spec(bias_rows.shape), _const_spec(w2.shape)]
        + [page_spec(g) for g in range(pages)] * 2,
        out_specs=seq,
        scratch_shapes=[pltpu.VMEM((rows, CHUNK), F32), pltpu.VMEM((rows, width), F32)],
    )
    return pl.pallas_call(
        functools.partial(_sb_sample_body, pages=pages, heads=heads, steps=steps),
        grid_spec=grid_spec,
        out_shape=jax.ShapeDtypeStruct((bd, nt, width), F32),
        compiler_params=_cparams(("arbitrary", "arbitrary")),
        name="sb_sample",
    )(page_table, q, kn, vn, bias_rows, w2, *([cache_kt] * pages), *([cache_vt] * pages))


def _ssd_body(*refs, cl, nv, has_state, heads, ssmw, nseq=1, row0=0):
    if nseq > 1:
        for sq in range(nseq):
            one = lambda r: r.at[pl.ds(sq, 1)]
            views = (list(refs[:3]) + [one(r) for r in refs[3:5]] + list(refs[5:12])
                     + [one(r) for r in refs[12:14]] + [r.at[sq] for r in refs[14:]])
            _ssd_body(*views, cl=cl, nv=nv, has_state=has_state, heads=heads, ssmw=ssmw, row0=sq * nv)
        return
    if has_state:
        (xbc_ref, z_ref, dt_ref, cprev_ref, h0_ref, wconv_ref, bconv_ref, dtb_ref, alog_ref,
         dskip_ref, gssm_ref, y_ref, conv_ref, ssm_ref, xstage, zstage, dtstage) = refs
    else:
        (xbc_ref, z_ref, dt_ref, wconv_ref, bconv_ref, dtb_ref, alog_ref,
         dskip_ref, gssm_ref, y_ref, conv_ref, ssm_ref, ht_ref, xprev) = refs
    c = pl.program_id(1)
    last = pl.num_programs(1) - 1
    taps, convd = wconv_ref.shape
    ncol_s = ssmw // LANES
    hpg = heads // SSM_GROUPS
    ntile = cl // SUBLANES
    assert nv >= taps - 1 and D_STATE == LANES and 2 * HEAD_DIM == LANES
    time_of = lambda p: ntile * (p % SUBLANES) + p // SUBLANES
    row_of = lambda t: SUBLANES * (t % ntile) + t // ntile
    sub = lax.broadcasted_iota(jnp.int32, (SUBLANES, LANES), 0)
    reach = sorted({(a - k) % ntile for a in range(ntile) for k in range(1, taps) if a - k < 0})

    def staged(ref, stage):
        stage[...] = jnp.zeros_like(stage)
        for j in range(stage.shape[0]):
            stage[j, 0:nv, :] = ref[row0:row0 + nv, j * LANES:(j + 1) * LANES]
        return stage

    def tiles(src, j):
        return [src[j, pl.ds(a, SUBLANES, stride=ntile), :] for a in range(ntile)]

    if has_state:
        xsrc, zsrc, dsrc = staged(xbc_ref, xstage), staged(z_ref, zstage), staged(dt_ref, dtstage)
    else:
        xsrc, zsrc, dsrc = xbc_ref, z_ref, dt_ref

        @pl.when(c == 0)
        def _():
            xprev[...] = jnp.zeros_like(xprev)
            ht_ref[...] = jnp.zeros_like(ht_ref)

    def conv_col(j):
        lanes = slice(j * LANES, (j + 1) * LANES)
        cur = tiles(xsrc, j)
        if has_state:
            prev = {}
            for a2 in reach:
                tile = jnp.zeros((SUBLANES, LANES), F32)
                for s in range(SUBLANES):
                    t = ntile * (s - SUBLANES) + a2
                    if t >= -(taps - 1):
                        row = cprev_ref[0, SUBLANES + t:SUBLANES + t + 1, lanes]
                        tile = jnp.where(sub == s, jnp.broadcast_to(row, (SUBLANES, LANES)), tile)
                prev[a2] = tile
        else:
            prev = {a2: xprev[n, :, lanes] for n, a2 in enumerate(reach)}
        w = [wconv_ref[taps - 1 - k:taps - k, lanes] for k in range(taps)]
        pieces = []
        for a in range(ntile):
            acc = bconv_ref[:, lanes] + cur[a] * w[0]
            for k in range(1, taps):
                a2, r = (a - k) % ntile, -((a - k) // ntile)
                if r == 0:
                    sh = cur[a2]
                else:
                    sh = jnp.where(sub < r, pltpu.roll(prev[a2], r, axis=0), pltpu.roll(cur[a2], r, axis=0))
                acc = acc + sh * w[k]
            pieces.append(acc)
        conv_ref[0, :, lanes] = jnp.concatenate(
            [cur[t % ntile][t // ntile:t // ntile + 1] for t in range(nv - (taps - 1), nv)], axis=0)
        if not has_state:
            for n, a2 in enumerate(reach):
                xprev[n, :, lanes] = cur[a2]
        return _silu(jnp.concatenate(pieces, axis=0))

    x = jnp.concatenate(tiles(dsrc, 0), axis=0) + dtb_ref[...]
    dt = jnp.maximum(x, 0.0) + jnp.log1p(jnp.exp(-jnp.abs(x)))
    if nv < cl:
        dt = jnp.where(time_of(lax.broadcasted_iota(jnp.int32, (cl, LANES), 0)) < nv, dt, 0.0)
    a = -jnp.exp(alog_ref[...])
    da = dt * a
    causal = (time_of(lax.broadcasted_iota(jnp.int32, (cl, cl), 1))
              <= time_of(lax.broadcasted_iota(jnp.int32, (cl, cl), 0)))
    ltri = jnp.where(causal, 1.0, 0.0).astype(BF16)
    acum = sum(_dot(ltri, part) for part in _split3(da))
    acum = acum * LOG2E
    acum_last = acum[cl - 1:cl, :]
    eac = jnp.exp2(acum)
    wgt = dt * jnp.exp2(acum_last - acum)
    cdec = jnp.exp2(acum_last)
    acum_t = acum.T
    src_t = acum_t - jnp.log2(dt.T)

    low = lax.broadcasted_iota(jnp.int32, (cl, LANES), 1) < HEAD_DIM
    low1 = low[:1]
    low_rows = lax.broadcasted_iota(jnp.int32, (LANES, D_STATE), 0) < HEAD_DIM

    ycols = []
    for g in range(SSM_GROUPS):
        bc = conv_col(ncol_s + g)
        cc = conv_col(ncol_s + SSM_GROUPS + g)
        bc_b = bc.astype(BF16)
        cb = _dot_nt(cc.astype(BF16), bc_b)
        if not has_state:
            bc_tb = bc.T.astype(BF16)
        for jp in range(hpg // 2):
            h0 = g * hpg + 2 * jp
            col = h0 // 2
            lanes = slice(col * LANES, (col + 1) * LANES)
            xs_p = conv_col(col)
            xs_pb = xs_p.astype(BF16)
            st = h0_ref[0, lanes, :] if has_state else ht_ref[:, lanes]
            st_b = st.astype(BF16)
            yh = []
            for hh in (h0, h0 + 1):
                seg = acum[:, hh:hh + 1] - src_t[hh:hh + 1, :]
                m = (cb * jnp.where(causal, jnp.exp2(seg), 0.0)).astype(BF16)
                off = (cc * eac[:, hh:hh + 1]).astype(BF16)
                if has_state:
                    yh.append(_dot(m, xs_pb) + _dot_nt(off, st_b))
                else:
                    yh.append(_dot(jnp.concatenate([m, off], axis=1), jnp.concatenate([xs_pb, st_b], axis=0)))
            xs_w = (xs_p * jnp.where(low, wgt[:, h0:h0 + 1], wgt[:, h0 + 1:h0 + 2])).astype(BF16)
            if has_state:
                s_new = lax.dot_general(xs_w, bc_b, (((0,), (0,)), ((), ())), preferred_element_type=F32)
                ssm_ref[0, lanes, :] = jnp.where(low_rows, cdec[:, h0:h0 + 1], cdec[:, h0 + 1:h0 + 2]) * st + s_new
            else:
                cd = jnp.where(low1, cdec[:, h0:h0 + 1], cdec[:, h0 + 1:h0 + 2])
                ht_ref[:, lanes] = cd * st + _dot(bc_tb, xs_w)
            zin = jnp.concatenate(tiles(zsrc, col), axis=0)
            ycols.append((jnp.where(low, yh[0], yh[1]) + dskip_ref[:, lanes] * xs_p) * _silu(zin))

    per_group = ncol_s // SSM_GROUPS
    for g in range(SSM_GROUPS):
        group = ycols[g * per_group:(g + 1) * per_group]
        ms = sum(jnp.sum(yc * yc, axis=-1, keepdims=True) for yc in group) * (1.0 / (per_group * LANES))
        scale = lax.rsqrt(ms + EPS)
        for n, yc in enumerate(group):
            col = g * per_group + n
            lanes = slice(col * LANES, (col + 1) * LANES)
            out = yc * scale * gssm_ref[:, lanes]
            if nv < cl:
                y_ref[row0:row0 + nv, lanes] = jnp.concatenate(
                    [out[row_of(t):row_of(t) + 1] for t in range(nv)], axis=0)
            else:
                for a in range(ntile):
                    y_ref[col, pl.ds(a, SUBLANES, stride=ntile), :] = out[a * SUBLANES:(a + 1) * SUBLANES]

    if not has_state:
        @pl.when(c == last)
        def _():
            ssm_ref[0] = ht_ref[...].T


def _ssd(xbc, z, dt, params, *, heads, cl, batch, seq_len, state=None):
    has_state = state is not None
    l = seq_len
    taps, convd = params[0].shape
    ssmw = heads * HEAD_DIM
    nseq = DECODE_ROWS // l if has_state else 1
    per_seq = lambda r, w: pl.BlockSpec((nseq, r, w), lambda bi, c: (bi, 0, 0))
    const = lambda a: _const_spec(a.shape)
    scratch = []
    if has_state:
        assert nseq * l == DECODE_ROWS and nseq > 1 and batch % nseq == 0
        nv, nchunks = l, 1
        blk = lambda w: pl.BlockSpec((nseq * nv, w), lambda bi, c: (bi, 0))
        y_shape = (batch * l, ssmw)
        in_specs = [blk(convd), blk(ssmw), blk(LANES), per_seq(SUBLANES, convd), per_seq(ssmw, D_STATE)]
        args = [xbc, z, dt] + list(state)
        scratch += [pltpu.VMEM((nseq, w // LANES, cl, LANES), F32) for w in (convd, ssmw, LANES)]
    else:
        nv, nchunks = cl, l // cl
        blk = lambda w: pl.BlockSpec((w // LANES, cl, LANES), lambda bi, c: (0, bi * nchunks + c, 0))
        y_shape = (ssmw // LANES, batch * l, LANES)
        in_specs = [blk(convd), blk(ssmw), pl.BlockSpec((1, cl, LANES), lambda bi, c: (bi, c, 0))]
        args = [xbc, z, dt]
        scratch += [pltpu.VMEM((D_STATE, ssmw), F32),
                    pltpu.VMEM((min(taps - 1, cl // SUBLANES), SUBLANES, convd), F32)]
    in_specs += [const(p) for p in params]
    args += list(params)
    return pl.pallas_call(
        functools.partial(_ssd_body, cl=cl, nv=nv, has_state=has_state, heads=heads, ssmw=ssmw, nseq=nseq),
        grid=(batch // nseq, nchunks),
        in_specs=in_specs,
        out_specs=[blk(ssmw), per_seq(taps - 1, convd), per_seq(ssmw, D_STATE)],
        out_shape=[jax.ShapeDtypeStruct(y_shape, F32),
                   jax.ShapeDtypeStruct((batch, taps - 1, convd), F32),
                   jax.ShapeDtypeStruct((batch, ssmw, D_STATE), F32)],
        scratch_shapes=scratch,
        compiler_params=_cparams(("arbitrary", "arbitrary")),
        name="ssd_sample" if has_state else "ssd_prompt",
    )(*args)


def _mix_out(o_sb, y_ref, x, gsb_ref, wsb_ref, wy_ref):
    o = _rms(o_sb, gsb_ref[...]).astype(BF16)
    if len(y_ref.shape) == 3:
        y = jnp.concatenate([y_ref[j] for j in range(y_ref.shape[0])], axis=1)
    else:
        y = y_ref[...]
    return x + _dot(o, wsb_ref[...]) + _dot(y.astype(BF16), wy_ref[...])


def _outproj_body(o_ref, y_ref, x_ref, gsb_ref, wsb_ref, wy_ref, x1_ref):
    x1_ref[...] = _mix_out(o_ref[...], y_ref, x_ref[...], gsb_ref, wsb_ref, wy_ref)


def _outproj(o_sb, y, x2d, gsb, wsb, wy, *, tm):
    t, d = x2d.shape
    row = lambda w: pl.BlockSpec((tm, w), lambda i: (i, 0))
    y_spec = row(y.shape[1]) if y.ndim == 2 else pl.BlockSpec((y.shape[0], tm, LANES), lambda i: (0, i, 0))
    return pl.pallas_call(
        _outproj_body,
        grid=(t // tm,),
        in_specs=[row(o_sb.shape[1]), y_spec, row(d), _const_spec(gsb.shape),
                  _const_spec(wsb.shape), _const_spec(wy.shape)],
        out_specs=row(d),
        out_shape=jax.ShapeDtypeStruct((t, d), F32),
        compiler_params=_cparams(("arbitrary",)),
        name="outproj",
    )(o_sb, y, x2d, gsb, wsb, wy)


def _ffn_tail(x1, gate, g1, g2, val, wc_ref, bc_ref, wdown_ref, gfin_ref):
    gc = bc_ref[...] + wc_ref[2:3, :] * gate + wc_ref[1:2, :] * g1 + wc_ref[0:1, :] * g2
    u = (_silu(gc) * val).astype(BF16)
    return _rms(x1 + _dot(u, wdown_ref[...]), gfin_ref[...])


def _ffn_prompt_body(o_ref, ymix_ref, x_ref, gsb_ref, wsb_ref, wy_ref, gffn_ref, wup_ref, wc_ref, bc_ref,
                     wdown_ref, gfin_ref, y_ref, fnew_ref, gbuf, *, tm, dff):
    j = pl.program_id(1)

    @pl.when(j == 0)
    def _():
        gbuf[0:SUBLANES, :] = jnp.zeros((SUBLANES, dff), F32)

    x1 = _mix_out(o_ref[0], ymix_ref, x_ref[0], gsb_ref, wsb_ref, wy_ref)
    up =_dot(_rms(x1, gffn_ref[...]).astype(BF16), wup_ref[...])
    gate = up[:, :dff]
    gbuf[SUBLANES:SUBLANES + tm, :] = gate
    g1 = gbuf[SUBLANES - 1:SUBLANES - 1 + tm, :]
    g2 = gbuf[SUBLANES - 2:SUBLANES - 2 + tm, :]
    y_ref[0] = _ffn_tail(x1, gate, g1, g2, up[:, dff:], wc_ref, bc_ref, wdown_ref, gfin_ref)
    gbuf[0:SUBLANES, :] = gbuf[tm:tm + SUBLANES, :]

    @pl.when(j == pl.num_programs(1) - 1)
    def _():
        fnew_ref[0] = gbuf[SUBLANES - 2:SUBLANES, :]


def _ffn_prompt(o_sb, y_t, x, mix_w, gffn, wup, wc, bc, wdown, gfin, *, tm):
    b, l, d = x.shape
    dff = wdown.shape[0]
    nb = l // tm
    blk = pl.BlockSpec((1, tm, d), lambda bi, j: (bi, j, 0))
    const = lambda a: _const_spec(a.shape)
    consts = tuple(mix_w) + (gffn, wup, wc, bc, wdown, gfin)
    return pl.pallas_call(
        functools.partial(_ffn_prompt_body, tm=tm, dff=dff),
        grid=(b, nb),
        in_specs=[pl.BlockSpec((1, tm, o_sb.shape[2]), lambda bi, j: (bi, j, 0)),
                  pl.BlockSpec((y_t.shape[0], tm, LANES), lambda bi, j: (0, bi * nb + j, 0)),
                  blk] + [const(a) for a in consts],
        out_specs=[blk, pl.BlockSpec((1, 2, dff), lambda bi, j: (bi, 0, 0))],
        out_shape=[jax.ShapeDtypeStruct((b, l, d), F32), jax.ShapeDtypeStruct((b, 2, dff), F32)],
        scratch_shapes=[pltpu.VMEM((tm + SUBLANES, dff), F32)],
        compiler_params=_cparams(("arbitrary", "arbitrary")),
        name="ffn_prompt",
    )(o_sb, y_t, x, *consts)


def _ffn_sample_body(x1_ref, p1_ref, p2_ref, gffn_ref, wup_ref, wc_ref, bc_ref, wdown_ref, gfin_ref,
                     y_ref, gate_ref, *, seq, dff):
    x1 = x1_ref[...]
    up = _dot(_rms(x1, gffn_ref[...]).astype(BF16), wup_ref[...])
    gate = up[:, :dff]
    gate_ref[...] = gate
    pos = lax.broadcasted_iota(jnp.int32, (x1.shape[0], 1), 0) % seq
    g1 = jnp.where(pos >= 1, pltpu.roll(gate, 1, axis=0), p1_ref[...])
    g2 = jnp.where(pos >= 2, pltpu.roll(gate, 2, axis=0), p2_ref[...])
    y_ref[...] = _ffn_tail(x1, gate, g1, g2, up[:, dff:], wc_ref, bc_ref, wdown_ref, gfin_ref)


def _ffn_sample(x1, p1, p2, gffn, wup, wc, bc, wdown, gfin, *, seq, tm):
    t, d = x1.shape
    dff = wdown.shape[0]
    row = lambda w: pl.BlockSpec((tm, w), lambda i: (i, 0))
    return pl.pallas_call(
        functools.partial(_ffn_sample_body, seq=seq, dff=dff),
        grid=(t // tm,),
        in_specs=[row(d), row(dff), row(dff)] + [_const_spec(a.shape) for a in (gffn, wup, wc, bc, wdown, gfin)],
        out_specs=[row(d), row(dff)],
        out_shape=[jax.ShapeDtypeStruct((t, d), F32), jax.ShapeDtypeStruct((t, dff), F32)],
        compiler_params=_cparams(("arbitrary",)),
        name="ffn_sample",
    )(x1, p1, p2, gffn, wup, wc, bc, wdown, gfin)


def _pad_lanes(a):
    return jnp.pad(a, ((0, 0), (0, LANES - a.shape[1])))


def kernel(x_prompt, x_sample, cache_k, cache_v, state_ssm, state_conv, state_ffn_conv, page_table,
           g_mix, w_in, sb_bias, g_sb_out, w_conv, b_conv, dt_bias, a_log, d_skip, g_ssm, w_out,
           g_ffn, w_up, w_ffn_conv, b_ffn_conv, w_down, g_final):
    assert g_mix.shape[0] == 1, "single-layer step"
    bp, lp, d = x_prompt.shape
    bd, ld, _ = x_sample.shape
    n_pool, page, _, sb_heads, _ = cache_k.shape
    n_pages = page_table.shape[1]
    assert page == CHUNK
    sbw = sb_heads * HEAD_DIM
    heads = a_log.shape[1]
    ssmw = heads * HEAD_DIM
    convd = w_conv.shape[2]
    dff = w_down.shape[1]

    w = w_in[0]
    wqkv = w[:, :3 * sbw].astype(BF16)
    wz = w[:, 3 * sbw:3 * sbw + ssmw].astype(BF16)
    wxbc = w[:, 3 * sbw + ssmw:3 * sbw + ssmw + convd].astype(BF16)
    wdt = _pad_lanes(w[:, 3 * sbw + ssmw + convd:]).astype(BF16)
    wsb = w_out[0, :sbw].astype(BF16)
    wy = w_out[0, sbw:].astype(BF16)
    wup = w_up[0].astype(BF16)
    wdown = w_down[0].astype(BF16)
    ssd_params = (w_conv[0], b_conv, _pad_lanes(dt_bias), _pad_lanes(a_log),
                  jnp.repeat(d_skip, HEAD_DIM, axis=1), g_ssm)
    w2 = _suffix_matrix()
    bias2 = sb_bias[0] * LOG2E
    npairs = sbw // LANES
    parts = jnp.stack(_split3(bias2)).reshape(3, npairs, 2, 1)
    btile = jnp.zeros((npairs, LANES, 2 * CHUNK), BF16).at[:, :3].set(
        jnp.broadcast_to(parts, (3, npairs, 2, CHUNK)).reshape(3, npairs, 2 * CHUNK).transpose(1, 0, 2))

    wkvt = w[:, sbw:3 * sbw].T.astype(BF16)

    def mix_in(x2d, tm, seq_len=None):
        return _inproj(x2d, g_mix, wqkv, wz, wxbc, wdt, wkvt, sbw=sbw, tm=min(tm, x2d.shape[0]),
                       seq_len=seq_len)

    xp2 = x_prompt.reshape(bp * lp, d)
    q, v_tok, z, xbc, dt, kt_p, vt_p = mix_in(xp2, 512, seq_len=lp)
    to_seq = lambda a: a.reshape(bp, lp, a.shape[-1])
    o_sb = _sb_prompt(to_seq(q), kt_p, to_seq(v_tok), btile, w2)
    y, conv_p, ssm_p = _ssd(xbc, z, to_seq(dt), ssd_params, heads=heads, cl=CHUNK, batch=bp, seq_len=lp)
    y_prompt, ffn_p = _ffn_prompt(o_sb, y, x_prompt, (g_sb_out, wsb, wy), g_ffn, wup, w_ffn_conv[0], b_ffn_conv,
                                  wdown, g_final[None], tm=512)

    ts = bd * ld
    xs2 = x_sample.reshape(ts, d)
    q, v_s, z, xbc, dt, k_s = mix_in(xs2, 256)
    to_seq = lambda a: a.reshape(bd, ld, a.shape[-1])
    bias_rows = jnp.broadcast_to(jnp.tile(bias2, ld)[:, None], (ld * sb_heads, 2 * CHUNK))
    pool_t = lambda c: jnp.transpose(c[:, :, 0], (0, 2, 3, 1)).reshape(n_pool, sbw, page)
    o_rows = _sb_sample(to_seq(q), to_seq(k_s), to_seq(v_s), bias_rows, w2, pool_t(cache_k), pool_t(cache_v),
                        page_table, heads=sb_heads, pages=max(g for g in (2, 4, 8, 16, 32) if n_pages % g == 0))
    cl_s = -(-ld // BF16_ROWS) * BF16_ROWS
    y, conv_s, ssm_s = _ssd(xbc, z, dt, ssd_params, heads=heads, cl=cl_s, batch=bd, seq_len=ld,
                            state=(jnp.pad(state_conv[:, 0], ((0, 0), (SUBLANES - state_conv.shape[2], 0), (0, 0))),
                                   state_ssm.reshape(bd, ssmw, D_STATE)))
    x1 = _outproj(o_rows.reshape(ts, sbw), y, xs2, g_sb_out, wsb, wy, tm=ts)
    prev = state_ffn_conv[:, 0]
    p1 = jnp.pad(prev[:, 1:2], ((0, 0), (0, ld - 1), (0, 0))).reshape(ts, dff)
    p2 = jnp.pad(prev, ((0, 0), (0, ld - 2), (0, 0))).reshape(ts, dff)
    y_sample, gate = _ffn_sample(x1, p1, p2, g_ffn, wup, w_ffn_conv[0], b_ffn_conv, wdown, g_final[None],
                                 seq=ld, tm=ts)
    ffn_s = gate.reshape(bd, ld, dff)[:, ld - 2:]

    kv_p = lambda a: jnp.transpose(a.reshape(bp, 1, sb_heads, HEAD_DIM, lp), (0, 4, 1, 2, 3))
    kv_s = lambda a: a.reshape(bd, ld, 1, sb_heads, HEAD_DIM)
    return (y_prompt, y_sample.reshape(bd, ld, d),
            kv_p(kt_p), kv_p(vt_p), ssm_p.reshape(bp, 1, heads, HEAD_DIM, D_STATE),
            conv_p[:, None], ffn_p[:, None],
            kv_s(k_s), kv_s(v_s), ssm_s.reshape(bd, 1, heads, HEAD_DIM, D_STATE),
            conv_s[:, None], ffn_s[:, None])
```
